```python
import jax, jax.numpy as jnp
from jax import lax
import numpy as np

D_MODEL = 4096
BATCH = 4
SEQ = 2048
DEPTH = 1

MIX_WIDTH = D_MODEL
HEAD_DIM = 128
CONV_WIDTH = MIX_WIDTH // 2
GMLP_WIDTH = MIX_WIDTH - CONV_WIDTH
CONV_GROUPS = CONV_WIDTH // HEAD_DIM
GMLP_HEADS = GMLP_WIDTH // HEAD_DIM
CONV_K = 3
CHUNK = 128
D_FF = 4 * D_MODEL
IN_PROJ_WIDTH = 3 * CONV_WIDTH + 2 * GMLP_WIDTH
EPS = 1e-5

kernel_name = "hybrid_shortconv_gmlp_block"


def rmsnorm(x, g):
    xf = x.astype(jnp.float32)
    inv = lax.rsqrt(jnp.mean(xf * xf, axis=-1, keepdims=True) + EPS)
    return (xf * inv * g.astype(jnp.float32)).astype(x.dtype)


def short_conv_mixer(b_gate, c_gate, h_in, conv_w):
    h = c_gate * h_in
    S = h.shape[1]
    hp = jnp.pad(h, ((0, 0), (CONV_K - 1, 0), (0, 0)))
    y = conv_w[0] * hp[:, 0:S]
    for k in range(1, CONV_K):
        y = y + conv_w[k] * hp[:, k:k + S]
    return b_gate * y


def chunked_spatial_gating(u, v, spatial_w, spatial_b):
    bsz, S, _ = v.shape
    n_chunks = S // CHUNK
    causal = jnp.tril(jnp.ones((CHUNK, CHUNK), dtype=bool))
    w = jnp.where(causal[None], spatial_w, jnp.zeros((), spatial_w.dtype))
    vc = v.reshape(bsz, n_chunks, CHUNK, GMLP_HEADS, HEAD_DIM)
    s = jnp.einsum('hts,bcshd->bcthd', w, vc) + spatial_b.T[None, None, :, :, None]
    return u * s.reshape(bsz, S, GMLP_WIDTH)


def setup_inputs(seed: int = 0) -> dict:
    key = jax.random.key(seed)
    ks = jax.random.split(key, 13)
    f32 = jnp.float32
    x = jax.random.normal(ks[0], (BATCH, SEQ, D_MODEL), f32)
    mix_norm_g = 1.0 + 0.02 * jax.random.normal(ks[1], (DEPTH, D_MODEL), f32)
    w_in = jax.random.normal(ks[2], (DEPTH, D_MODEL, IN_PROJ_WIDTH), f32) * D_MODEL ** -0.5
    conv_w = jax.random.normal(ks[3], (DEPTH, CONV_K, CONV_WIDTH), f32) * CONV_K ** -0.5
    spatial_w = jax.random.normal(ks[4], (DEPTH, GMLP_HEADS, CHUNK, CHUNK), f32) * (0.5 * CHUNK ** -0.5)
    spatial_b = 1.0 + 0.02 * jax.random.normal(ks[5], (DEPTH, GMLP_HEADS, CHUNK), f32)
    conv_out_norm_g = 1.0 + 0.02 * jax.random.normal(ks[6], (DEPTH, CONV_WIDTH), f32)
    gmlp_out_norm_g = 1.0 + 0.02 * jax.random.normal(ks[7], (DEPTH, GMLP_WIDTH), f32)
    w_out = jax.random.normal(ks[8], (DEPTH, MIX_WIDTH, D_MODEL), f32) * MIX_WIDTH ** -0.5
    mlp_norm_g = 1.0 + 0.02 * jax.random.normal(ks[9], (DEPTH, D_MODEL), f32)
    w_up = jax.random.normal(ks[10], (DEPTH, D_MODEL, D_FF), f32) * D_MODEL ** -0.5
    w_down = jax.random.normal(ks[11], (DEPTH, D_FF, D_MODEL), f32) * D_FF ** -0.5
    final_norm_g = 1.0 + 0.02 * jax.random.normal(ks[12], (D_MODEL,), f32)
    return {"x": x, "mix_norm_g": mix_norm_g, "w_in": w_in, "conv_w": conv_w,
            "spatial_w": spatial_w, "spatial_b": spatial_b,
            "conv_out_norm_g": conv_out_norm_g, "gmlp_out_norm_g": gmlp_out_norm_g,
            "w_out": w_out, "mlp_norm_g": mlp_norm_g, "w_up": w_up, "w_down": w_down,
            "final_norm_g": final_norm_g}


def reference(x, mix_norm_g, w_in, conv_w, spatial_w, spatial_b, conv_out_norm_g,
              gmlp_out_norm_g, w_out, mlp_norm_g, w_up, w_down, final_norm_g):
    split_points = [CONV_WIDTH, 2 * CONV_WIDTH, 3 * CONV_WIDTH, 3 * CONV_WIDTH + GMLP_WIDTH]
    h = x
    for l in range(DEPTH):
        xn = rmsnorm(h, mix_norm_g[l])
        proj = jnp.einsum('bsd,de->bse', xn, w_in[l])
        b_gate, c_gate, h_in, u, v = jnp.split(proj, split_points, axis=-1)
        y_a = short_conv_mixer(b_gate, c_gate, h_in, conv_w[l])
        y_b = chunked_spatial_gating(jax.nn.gelu(u), jax.nn.gelu(v),
                                     spatial_w[l], spatial_b[l])
        y = jnp.concatenate([rmsnorm(y_a, conv_out_norm_g[l]),
                             rmsnorm(y_b, gmlp_out_norm_g[l])], axis=-1)
        h = h + jnp.einsum('bse,ed->bsd', y, w_out[l])
        xn = rmsnorm(h, mlp_norm_g[l])
        a = jnp.square(jax.nn.relu(jnp.einsum('bsd,df->bsf', xn, w_up[l])))
        h = h + jnp.einsum('bsf,fd->bsd', a, w_down[l])
    return rmsnorm(h, final_norm_g)
```

```python
import functools

import jax
import jax.numpy as jnp
from jax import lax
from jax.experimental import pallas as pl
from jax.experimental.pallas import tpu as pltpu

EPS = 1e-5
HEAD_DIM = 128
CHUNK = 128
CONV_K = 3
LANES = 128
HALO = 8

_VMEM_LIMIT = 56 * 1024 * 1024

_ARB2 = ("arbitrary", "arbitrary")


def _params(sem):
    return pltpu.CompilerParams(dimension_semantics=sem, vmem_limit_bytes=_VMEM_LIMIT)


def _bdot(a, b):
    return jnp.dot(a.astype(jnp.bfloat16), b.astype(jnp.bfloat16),
                   preferred_element_type=jnp.float32)


def _lane_group_sum(sq):
    acc = sq[:, 0:LANES]
    for c in range(1, sq.shape[1] // LANES):
        acc = acc + sq[:, c * LANES:(c + 1) * LANES]
    return acc


def _rmsnorm_kernel(x_ref, g_ref, o_ref):
    x = x_ref[...]
    ms = jnp.mean(x * x, axis=-1, keepdims=True)
    o_ref[...] = (x * lax.rsqrt(ms + EPS) * g_ref[...]).astype(o_ref.dtype)


def _rmsnorm(x, g, out_dtype, tr=512):
    t, d = x.shape
    return pl.pallas_call(
        _rmsnorm_kernel,
        grid=(t // tr,),
        in_specs=[pl.BlockSpec((tr, d), lambda i: (i, 0)),
                  pl.BlockSpec((1, d), lambda i: (0, 0))],
        out_specs=pl.BlockSpec((tr, d), lambda i: (i, 0)),
        out_shape=jax.ShapeDtypeStruct((t, d), out_dtype),
        compiler_params=_params(("arbitrary",)),
        name="rmsnorm_cast",
    )(x, g.reshape(1, d))


def _conv_kernel(tiles_per_seq, x_ref, wb_ref, wc_ref, wh_ref, cw_ref, g_ref,
                 y_ref, ssq_ref, ext_ref):
    i = pl.program_id(1)
    tm = x_ref.shape[0]
    xb = x_ref[...]
    bg = _bdot(xb, wb_ref[...])
    ch = _bdot(xb, wc_ref[...]) * _bdot(xb, wh_ref[...])

    @pl.when(i % tiles_per_seq == 0)
    def _():
        ext_ref[0:HALO, :] = jnp.zeros((HALO, ext_ref.shape[1]), jnp.float32)

    ext_ref[HALO:HALO + tm, :] = ch
    conv = cw_ref[CONV_K - 1:CONV_K, :] * ch
    for k in range(CONV_K - 1):
        shift = CONV_K - 1 - k
        conv = conv + cw_ref[k:k + 1, :] * ext_ref[HALO - shift:HALO - shift + tm, :]
    ext_ref[0:HALO, :] = ch[tm - HALO:tm, :]

    ya = bg * conv
    ssq_ref[0] = _lane_group_sum(ya * ya)
    y_ref[...] = (ya * g_ref[...]).astype(y_ref.dtype)


def _conv_mixer(xn, w_in, conv_w, g_a, seq, cw, tm=512, tn=512):
    t, d = xn.shape
    nj = cw // tn
    kern = functools.partial(_conv_kernel, seq // tm)
    wspec = lambda off: pl.BlockSpec((d, tn), lambda j, i, off=off: (0, j + off))
    return pl.pallas_call(
        kern,
        grid=(nj, t // tm),
        in_specs=[pl.BlockSpec((tm, d), lambda j, i: (i, 0)),
                  wspec(0), wspec(nj), wspec(2 * nj),
                  pl.BlockSpec((CONV_K, tn), lambda j, i: (0, j)),
                  pl.BlockSpec((1, tn), lambda j, i: (0, j))],
        out_specs=[pl.BlockSpec((tm, tn), lambda j, i: (i, j)),
                   pl.BlockSpec((1, tm, LANES), lambda j, i: (j, i, 0))],
        out_shape=[jax.ShapeDtypeStruct((t, cw), jnp.bfloat16),
                   jax.ShapeDtypeStruct((nj, t, LANES), jnp.float32)],
        scratch_shapes=[pltpu.VMEM((HALO + tm, tn), jnp.float32)],
        compiler_params=_params(_ARB2),
        name="conv_mixer",
    )(xn, w_in, w_in, w_in, conv_w, g_a.reshape(1, cw))


def _gmlp_kernel(x_ref, wu_ref, wv_ref, sw_ref, sb_ref, g_ref, y_ref, ssq_ref, yb_ref):
    tm, tn = yb_ref.shape
    xb = x_ref[...]
    gu = jax.nn.gelu(_bdot(xb, wu_ref[...]))
    gv = jax.nn.gelu(_bdot(xb, wv_ref[...])).astype(jnp.bfloat16)
    row = lax.broadcasted_iota(jnp.int32, (CHUNK, CHUNK), 0)
    col = lax.broadcasted_iota(jnp.int32, (CHUNK, CHUNK), 1)
    causal = col <= row
    n_chunks = tm // CHUNK
    for h in range(tn // HEAD_DIM):
        hs = slice(h * HEAD_DIM, (h + 1) * HEAD_DIM)
        w = jnp.where(causal, sw_ref[h], 0.0)
        vcat = jnp.concatenate(
            [gv[c * CHUNK:(c + 1) * CHUNK, hs] for c in range(n_chunks)], axis=1)
        s = _bdot(w, vcat)
        for c in range(n_chunks):
            rs = slice(c * CHUNK, (c + 1) * CHUNK)
            yb_ref[rs, hs] = gu[rs, hs] * (s[:, c * CHUNK:(c + 1) * CHUNK] + sb_ref[h])
    yb = yb_ref[...]
    ssq_ref[0] = _lane_group_sum(yb * yb)
    y_ref[...] = (yb * g_ref[...]).astype(y_ref.dtype)


def _gmlp_mixer(xn, w_in, spatial_w, spatial_b, g_b, col0, gw, tm=512, tn=512):
    t, d = xn.shape
    nj = gw // tn
    hpt = tn // HEAD_DIM
    off_u = col0 // tn
    sb = jnp.broadcast_to(spatial_b[:, :, None], spatial_b.shape + (HEAD_DIM,))
    return pl.pallas_call(
        _gmlp_kernel,
        grid=(nj, t // tm),
        in_specs=[pl.BlockSpec((tm, d), lambda j, i: (i, 0)),
                  pl.BlockSpec((d, tn), lambda j, i: (0, j + off_u)),
                  pl.BlockSpec((d, tn), lambda j, i: (0, j + off_u + nj)),
                  pl.BlockSpec((hpt, CHUNK, CHUNK), lambda j, i: (j, 0, 0)),
                  pl.BlockSpec((hpt, CHUNK, HEAD_DIM), lambda j, i: (j, 0, 0)),
                  pl.BlockSpec((1, tn), lambda j, i: (0, j))],
        out_specs=[pl.BlockSpec((tm, tn), lambda j, i: (i, j)),
                   pl.BlockSpec((1, tm, LANES), lambda j, i: (j, i, 0))],
        out_shape=[jax.ShapeDtypeStruct((t, gw), jnp.bfloat16),
                   jax.ShapeDtypeStruct((nj, t, LANES), jnp.float32)],
        scratch_shapes=[pltpu.VMEM((tm, tn), jnp.float32)],
        compiler_params=_params(_ARB2),
        name="gmlp_mixer",
    )(xn, w_in, w_in, spatial_w, sb, g_b.reshape(1, gw))


def _inv_rms(ssq_ref, width):
    ssq = jnp.sum(ssq_ref[...], axis=0)
    return lax.rsqrt(jnp.sum(ssq, axis=-1, keepdims=True) / width + EPS)


def _out_proj_kernel(ya_ref, yb_ref, wa_ref, wb_ref, sa_ref, sb_ref, x_ref, o_ref):
    pa = _bdot(ya_ref[...], wa_ref[...]) * _inv_rms(sa_ref, ya_ref.shape[1])
    pb = _bdot(yb_ref[...], wb_ref[...]) * _inv_rms(sb_ref, yb_ref.shape[1])
    o_ref[...] = x_ref[...] + pa + pb


def _out_proj(ya, yb, w_out, ssq_a, ssq_b, x, tm=512, tn=1024):
    t, cw = ya.shape
    gw = yb.shape[1]
    d = w_out.shape[1]
    assert cw == gw
    return pl.pallas_call(
        _out_proj_kernel,
        grid=(d // tn, t // tm),
        in_specs=[pl.BlockSpec((tm, cw), lambda j, i: (i, 0)),
                  pl.BlockSpec((tm, gw), lambda j, i: (i, 0)),
                  pl.BlockSpec((cw, tn), lambda j, i: (0, j)),
                  pl.BlockSpec((gw, tn), lambda j, i: (1, j)),
                  pl.BlockSpec((ssq_a.shape[0], tm, LANES), lambda j, i: (0, i, 0)),
                  pl.BlockSpec((ssq_b.shape[0], tm, LANES), lambda j, i: (0, i, 0)),
                  pl.BlockSpec((tm, tn), lambda j, i: (i, j))],
        out_specs=pl.BlockSpec((tm, tn), lambda j, i: (i, j)),
        out_shape=jax.ShapeDtypeStruct((t, d), jnp.float32),
        compiler_params=_params(_ARB2),
        name="out_proj",
    )(ya, yb, w_out, w_out, ssq_a, ssq_b, x)


def _mlp_up_kernel(x_ref, w_ref, a_ref):
    r = jnp.maximum(_bdot(x_ref[...], w_ref[...]), 0.0)
    a_ref[...] = (r * r).astype(a_ref.dtype)


def _mlp_up(xn, w_up, tm=1024, tn=1024):
    t, d = xn.shape
    f = w_up.shape[1]
    return pl.pallas_call(
        _mlp_up_kernel,
        grid=(f // tn, t // tm),
        in_specs=[pl.BlockSpec((tm, d), lambda j, i: (i, 0)),
                  pl.BlockSpec((d, tn), lambda j, i: (0, j))],
        out_specs=pl.BlockSpec((tm, tn), lambda j, i: (i, j)),
        out_shape=jax.ShapeDtypeStruct((t, f), jnp.bfloat16),
        compiler_params=_params(_ARB2),
        name="mlp_up",
    )(xn, w_up)


def _mlp_down_kernel(a_ref, w_ref, h_ref, o_ref):
    @pl.when(pl.program_id(2) == 0)
    def _():
        o_ref[...] = h_ref[...]

    o_ref[...] += _bdot(a_ref[...], w_ref[...])


def _mlp_down(a, w_down, h, tm=1024, tn=1024, tk=2048):
    t, f = a.shape
    d = w_down.shape[1]
    return pl.pallas_call(
        _mlp_down_kernel,
        grid=(d // tn, t // tm, f // tk),
        in_specs=[pl.BlockSpec((tm, tk), lambda j, i, k: (i, k)),
                  pl.BlockSpec((tk, tn), lambda j, i, k: (k, j)),
                  pl.BlockSpec((tm, tn), lambda j, i, k: (i, j))],
        out_specs=pl.BlockSpec((tm, tn), lambda j, i, k: (i, j)),
        out_shape=jax.ShapeDtypeStruct((t, d), jnp.float32),
        compiler_params=_params(("arbitrary", "arbitrary", "arbitrary")),
        name="mlp_down",
    )(a, w_down, h)


def kernel(x, mix_norm_g, w_in, conv_w, spatial_w, spatial_b, conv_out_norm_g,
           gmlp_out_norm_g, w_out, mlp_norm_g, w_up, w_down, final_norm_g):
    bsz, seq, d = x.shape
    depth = w_in.shape[0]
    cw = conv_w.shape[2]
    gw = spatial_w.shape[1] * HEAD_DIM
    assert seq % CHUNK == 0 and w_in.shape[2] == 3 * cw + 2 * gw
    bf16 = jnp.bfloat16
    h = x.reshape(bsz * seq, d)
    for l in range(depth):
        w_in_l = w_in[l].astype(bf16)
        xn = _rmsnorm(h, mix_norm_g[l], bf16)
        ya, ssq_a = _conv_mixer(xn, w_in_l, conv_w[l], conv_out_norm_g[l], seq, cw)
        yb, ssq_b = _gmlp_mixer(xn, w_in_l, spatial_w[l], spatial_b[l], gmlp_out_norm_g[l],
                                3 * cw, gw)
        h = _out_proj(ya, yb, w_out[l].astype(bf16), ssq_a, ssq_b, h)
        xn = _rmsnorm(h, mlp_norm_g[l], bf16)
        a = _mlp_up(xn, w_up[l].astype(bf16))
        h = _mlp_down(a, w_down[l].astype(bf16), h)
    out = _rmsnorm(h, final_norm_g, x.dtype)
    return out.reshape(bsz, seq, d)
```

```python
import functools

import jax
import jax.numpy as jnp
from jax import lax
from jax.experimental import pallas as pl
from jax.experimental.pallas import tpu as pltpu

EPS = 1e-5
HEAD_DIM = 128
CHUNK = 128
CONV_K = 3
LANES = 128
HALO = 8

_VMEM_LIMIT = 60000 * 1024

_ARB2 = ("arbitrary", "arbitrary")


def _params(sem):
    return pltpu.CompilerParams(dimension_semantics=sem, vmem_limit_bytes=_VMEM_LIMIT)


def _bdot(a, b):
    return jnp.dot(a.astype(jnp.bfloat16), b.astype(jnp.bfloat16),
                   preferred_element_type=jnp.float32)


def _lane_group_sum(sq):
    acc = sq[:, 0:LANES]
    for c in range(1, sq.shape[1] // LANES):
        acc = acc + sq[:, c * LANES:(c + 1) * LANES]
    return acc


def _rep_lanes(v, width):
    return jnp.concatenate([v] * (width // LANES), axis=1)


def _zero_row_sums_at_start(acc_ref):
    @pl.when((pl.program_id(0) == 0) & (pl.program_id(1) == 0))
    def _():
        acc_ref[...] = jnp.zeros(acc_ref.shape, acc_ref.dtype)


def _inv_rms_spec(tm, nj):
    return pl.BlockSpec((tm, LANES), lambda j, i: (jnp.where(j == nj - 1, i, 0), 0))


def _accumulate_inv_rms(acc_ref, inv_ref, row0, sq, width):
    rc = sq.shape[0]
    tm = inv_ref.shape[0]
    rows = pl.ds(pl.multiple_of(pl.program_id(1) * tm + row0, rc), rc)
    ssq = acc_ref[rows, :] + sq
    acc_ref[rows, :] = ssq
    total = jnp.sum(ssq, axis=-1, keepdims=True)
    inv_ref[row0:row0 + rc, :] = jnp.broadcast_to(lax.rsqrt(total / width + EPS),
                                                  (rc, inv_ref.shape[1]))


def _rmsnorm_kernel(x_ref, g_ref, o_ref):
    x = x_ref[...]
    ms = jnp.mean(x * x, axis=-1, keepdims=True)
    o_ref[...] = (x * lax.rsqrt(ms + EPS) * g_ref[...]).astype(o_ref.dtype)


def _rmsnorm(x, g, out_dtype, tr=512):
    t, d = x.shape
    return pl.pallas_call(
        _rmsnorm_kernel,
        grid=(t // tr,),
        in_specs=[pl.BlockSpec((tr, d), lambda i: (i, 0)),
                  pl.BlockSpec((1, d), lambda i: (0, 0))],
        out_specs=pl.BlockSpec((tr, d), lambda i: (i, 0)),
        out_shape=jax.ShapeDtypeStruct((t, d), out_dtype),
        compiler_params=_params(("arbitrary",)),
        name="rmsnorm_cast",
    )(x, g.reshape(1, d))


def _cast_weights_once(w_refs, wbf_ref):
    @pl.when(pl.program_id(1) == 0)
    def _():
        for n, w_ref in enumerate(w_refs):
            wbf_ref[n] = w_ref[...].astype(wbf_ref.dtype)


def _conv_kernel(tiles_per_seq, rc, width, x_ref, wb_ref, wc_ref, wh_ref, cw_ref, g_ref,
                 y_ref, inv_ref, wbf_ref, ext_ref, acc_ref):
    i = pl.program_id(1)
    tm, tn = y_ref.shape
    _cast_weights_once((wb_ref, wc_ref, wh_ref), wbf_ref)

    @pl.when(i % tiles_per_seq == 0)
    def _():
        ext_ref[0:HALO, :] = jnp.zeros((HALO, tn), jnp.float32)

    _zero_row_sums_at_start(acc_ref)
    for r in range(tm // rc):
        rs = slice(r * rc, (r + 1) * rc)
        xb = x_ref[rs, :]
        bg = _bdot(xb, wbf_ref[0])
        ch = _bdot(xb, wbf_ref[1]) * _bdot(xb, wbf_ref[2])
        base = HALO + r * rc
        ext_ref[base:base + rc, :] = ch
        conv = cw_ref[CONV_K - 1:CONV_K, :] * ch
        for k in range(CONV_K - 1):
            shift = CONV_K - 1 - k
            conv = conv + cw_ref[k:k + 1, :] * ext_ref[base - shift:base - shift + rc, :]
        ya = bg * conv
        y_ref[rs, :] = (ya * g_ref[...]).astype(y_ref.dtype)
        _accumulate_inv_rms(acc_ref, inv_ref, r * rc, _lane_group_sum(ya * ya), width)
    ext_ref[0:HALO, :] = ext_ref[tm:tm + HALO, :]


def _conv_mixer(xn, w_in, conv_w, g_a, seq, cw, tm=1024, tn=256, rc=256):
    t, d = xn.shape
    nj = cw // tn
    kern = functools.partial(_conv_kernel, seq // tm, rc, cw)
    wspec = lambda off: pl.BlockSpec((d, tn), lambda j, i, off=off: (0, j + off))
    return pl.pallas_call(
        kern,
        grid=(nj, t // tm),
        in_specs=[pl.BlockSpec((tm, d), lambda j, i: (i, 0)),
                  wspec(0), wspec(nj), wspec(2 * nj),
                  pl.BlockSpec((CONV_K, tn), lambda j, i: (0, j)),
                  pl.BlockSpec((1, tn), lambda j, i: (0, j))],
        out_specs=[pl.BlockSpec((tm, tn), lambda j, i: (i, j)), _inv_rms_spec(tm, nj)],
        out_shape=[jax.ShapeDtypeStruct((t, cw), jnp.bfloat16),
                   jax.ShapeDtypeStruct((t, LANES), jnp.float32)],
        scratch_shapes=[pltpu.VMEM((3, d, tn), jnp.bfloat16),
                        pltpu.VMEM((HALO + tm, tn), jnp.float32),
                        pltpu.VMEM((t, LANES), jnp.float32)],
        compiler_params=_params(_ARB2),
        name="conv_mixer",
    )(xn, w_in, w_in, w_in, conv_w, g_a.reshape(1, cw))


def _gmlp_kernel(rc, width, x_ref, wu_ref, wv_ref, sw_ref, sb_ref, g_ref,
                 y_ref, inv_ref, wbf_ref, acc_ref):
    tm, tn = y_ref.shape
    _cast_weights_once((wu_ref, wv_ref), wbf_ref)
    row = lax.broadcasted_iota(jnp.int32, (CHUNK, CHUNK), 0)
    col = lax.broadcasted_iota(jnp.int32, (CHUNK, CHUNK), 1)
    causal = col <= row
    heads = tn // HEAD_DIM
    ws = [jnp.where(causal, sw_ref[h], 0.0).astype(jnp.bfloat16) for h in range(heads)]
    n_chunks = rc // CHUNK
    n_rc = tm // rc
    _zero_row_sums_at_start(acc_ref)

    def uv_dots(r):
        xb = x_ref[r * rc:(r + 1) * rc, :]
        return _bdot(xb, wbf_ref[0]), _bdot(xb, wbf_ref[1])

    uv = uv_dots(0)
    for r in range(n_rc):
        rs = slice(r * rc, (r + 1) * rc)
        u, v = uv
        if r + 1 < n_rc:
            uv = uv_dots(r + 1)
        gu = jax.nn.gelu(u)
        gv = jax.nn.gelu(v).astype(jnp.bfloat16)
        gate = []
        for h in range(heads):
            hs = slice(h * HEAD_DIM, (h + 1) * HEAD_DIM)
            vcat = jnp.concatenate(
                [gv[c * CHUNK:(c + 1) * CHUNK, hs] for c in range(n_chunks)], axis=1)
            gate.append(_bdot(ws[h], vcat))
        s = jnp.concatenate(
            [jnp.concatenate([gate[h][:, c * CHUNK:(c + 1) * CHUNK] + sb_ref[h]
                              for h in range(heads)], axis=1)
             for c in range(n_chunks)], axis=0)
        yb = gu * s
        y_ref[rs, :] = (yb * g_ref[...]).astype(y_ref.dtype)
        _accumulate_inv_rms(acc_ref, inv_ref, r * rc, _lane_group_sum(yb * yb), width)


def _gmlp_mixer(xn, w_in, spatial_w, spatial_b, g_b, col0, gw, tm=1024, tn=256, rc=256):
    t, d = xn.shape
    nj = gw // tn
    hpt = tn // HEAD_DIM
    off_u = col0 // tn
    sb = jnp.broadcast_to(spatial_b[:, :, None], spatial_b.shape + (HEAD_DIM,))
    kern = functools.partial(_gmlp_kernel, rc, gw)
    return pl.pallas_call(
        kern,
        grid=(nj, t // tm),
        in_specs=[pl.BlockSpec((tm, d), lambda j, i: (i, 0)),
                  pl.BlockSpec((d, tn), lambda j, i: (0, j + off_u)),
                  pl.BlockSpec((d, tn), lambda j, i: (0, j + off_u + nj)),
                  pl.BlockSpec((hpt, CHUNK, CHUNK), lambda j, i: (j, 0, 0)),
                  pl.BlockSpec((hpt, CHUNK, HEAD_DIM), lambda j, i: (j, 0, 0)),
                  pl.BlockSpec((1, tn), lambda j, i: (0, j))],
        out_specs=[pl.BlockSpec((tm, tn), lambda j, i: (i, j)), _inv_rms_spec(tm, nj)],
        out_shape=[jax.ShapeDtypeStruct((t, gw), jnp.bfloat16),
                   jax.ShapeDtypeStruct((t, LANES), jnp.float32)],
        scratch_shapes=[pltpu.VMEM((2, d, tn), jnp.bfloat16),
                        pltpu.VMEM((t, LANES), jnp.float32)],
        compiler_params=_params(_ARB2),
        name="gmlp_mixer",
    )(xn, w_in, w_in, spatial_w, sb, g_b.reshape(1, gw))


def _out_proj_kernel(rc, width, ya_ref, yb_ref, wa_ref, wb_ref, ia_ref, ib_ref, x_ref, g_ref,
                     h_ref, xg_ref, inv_ref, wbf_ref, acc_ref):
    tm, tn = h_ref.shape
    _cast_weights_once((wa_ref, wb_ref), wbf_ref)
    _zero_row_sums_at_start(acc_ref)
    for r in range(tm // rc):
        rs = slice(r * rc, (r + 1) * rc)
        pa = _bdot(ya_ref[rs, :], wbf_ref[0]) * _rep_lanes(ia_ref[rs, :], tn)
        pb = _bdot(yb_ref[rs, :], wbf_ref[1]) * _rep_lanes(ib_ref[rs, :], tn)
        h = x_ref[rs, :] + pa + pb
        h_ref[rs, :] = h
        xg_ref[rs, :] = (h * g_ref[...]).astype(xg_ref.dtype)
        _accumulate_inv_rms(acc_ref, inv_ref, r * rc, _lane_group_sum(h * h), width)


def _out_proj(ya, yb, w_out, inv_a, inv_b, x, g_mlp, tm=1024, tn=512, rc=256):
    t, cw = ya.shape
    gw = yb.shape[1]
    d = w_out.shape[1]
    assert cw == gw
    kern = functools.partial(_out_proj_kernel, rc, d)
    row_spec = pl.BlockSpec((tm, LANES), lambda j, i: (i, 0))
    tile_spec = pl.BlockSpec((tm, tn), lambda j, i: (i, j))
    return pl.pallas_call(
        kern,
        grid=(d // tn, t // tm),
        in_specs=[pl.BlockSpec((tm, cw), lambda j, i: (i, 0)),
                  pl.BlockSpec((tm, gw), lambda j, i: (i, 0)),
                  pl.BlockSpec((cw, tn), lambda j, i: (0, j)),
                  pl.BlockSpec((gw, tn), lambda j, i: (1, j)),
                  row_spec, row_spec, tile_spec,
                  pl.BlockSpec((1, tn), lambda j, i: (0, j))],
        out_specs=[tile_spec, tile_spec, _inv_rms_spec(tm, d // tn)],
        out_shape=[jax.ShapeDtypeStruct((t, d), jnp.float32),
                   jax.ShapeDtypeStruct((t, d), jnp.bfloat16),
                   jax.ShapeDtypeStruct((t, LANES), jnp.float32)],
        scratch_shapes=[pltpu.VMEM((2, cw, tn), jnp.bfloat16),
                        pltpu.VMEM((t, LANES), jnp.float32)],
        compiler_params=_params(_ARB2),
        name="out_proj",
    )(ya, yb, w_out, w_out, inv_a, inv_b, x, g_mlp.reshape(1, d))


def _mlp_up_kernel(rc, x_ref, w_ref, wd_ref, r_ref, wdb_ref):
    wdb_ref[...] = wd_ref[...].astype(wdb_ref.dtype)
    w = w_ref[...].astype(jnp.bfloat16)
    for r in range(r_ref.shape[0] // rc):
        rs = slice(r * rc, (r + 1) * rc)
        z = jnp.maximum(_bdot(x_ref[rs, :], w), 0.0)
        r_ref[rs, :] = (z * z).astype(r_ref.dtype)


def _mlp_up(xg, w_up, w_down, tm=1024, tn=512, rc=512):
    t, d = xg.shape
    f = w_up.shape[1]
    nj, ni = f // tn, t // tm
    slab = f // (nj * ni)
    kern = functools.partial(_mlp_up_kernel, rc)
    slab_spec = pl.BlockSpec((slab, d), lambda j, i: (j * ni + i, 0))
    return pl.pallas_call(
        kern,
        grid=(nj, ni),
        in_specs=[pl.BlockSpec((tm, d), lambda j, i: (i, 0)),
                  pl.BlockSpec((d, tn), lambda j, i: (0, j)),
                  slab_spec],
        out_specs=[pl.BlockSpec((tm, tn), lambda j, i: (i, j)), slab_spec],
        out_shape=[jax.ShapeDtypeStruct((t, f), jnp.bfloat16),
                   jax.ShapeDtypeStruct(w_down.shape, jnp.bfloat16)],
        compiler_params=_params(_ARB2),
        name="mlp_up",
    )(xg, w_up, w_down)


def _mlp_down_kernel(r_ref, w_ref, h_ref, inv_ref, o_ref):
    k = pl.program_id(2)

    @pl.when(k == 0)
    def _():
        o_ref[...] = jnp.zeros(o_ref.shape, o_ref.dtype)

    o_ref[...] += _bdot(r_ref[...], w_ref[...])

    @pl.when(k == pl.num_programs(2) - 1)
    def _():
        inv = _rep_lanes(inv_ref[...], o_ref.shape[1])
        o_ref[...] = h_ref[...] + inv * inv * o_ref[...]


def _mlp_down(r, w_down, h, inv, tm=1024, tn=1024, tk=4096):
    t, f = r.shape
    d = w_down.shape[1]
    return pl.pallas_call(
        _mlp_down_kernel,
        grid=(d // tn, t // tm, f // tk),
        in_specs=[pl.BlockSpec((tm, tk), lambda j, i, k: (i, k)),
                  pl.BlockSpec((tk, tn), lambda j, i, k: (k, j)),
                  pl.BlockSpec((tm, tn), lambda j, i, k: (i, j)),
                  pl.BlockSpec((tm, LANES), lambda j, i, k: (i, 0))],
        out_specs=pl.BlockSpec((tm, tn), lambda j, i, k: (i, j)),
        out_shape=jax.ShapeDtypeStruct((t, d), jnp.float32),
        compiler_params=_params(("arbitrary", "arbitrary", "arbitrary")),
        name="mlp_down",
    )(r, w_down, h, inv)


def kernel(x, mix_norm_g, w_in, conv_w, spatial_w, spatial_b, conv_out_norm_g,
           gmlp_out_norm_g, w_out, mlp_norm_g, w_up, w_down, final_norm_g):
    bsz, seq, d = x.shape
    depth = w_in.shape[0]
    cw = conv_w.shape[2]
    gw = spatial_w.shape[1] * HEAD_DIM
    assert seq % CHUNK == 0 and w_in.shape[2] == 3 * cw + 2 * gw
    h = x.reshape(bsz * seq, d)
    for l in range(depth):
        xn = _rmsnorm(h, mix_norm_g[l], jnp.bfloat16)
        ya, inv_a = _conv_mixer(xn, w_in[l], conv_w[l], conv_out_norm_g[l], seq, cw)
        yb, inv_b = _gmlp_mixer(xn, w_in[l], spatial_w[l], spatial_b[l], gmlp_out_norm_g[l],
                                3 * cw, gw)
        h, xg, inv_h = _out_proj(ya, yb, w_out[l], inv_a, inv_b, h, mlp_norm_g[l])
        r, w_down_bf16 = _mlp_up(xg, w_up[l], w_down[l])
        h = _mlp_down(r, w_down_bf16, h, inv_h)
    out = _rmsnorm(h, final_norm_g, x.dtype)
    return out.reshape(bsz, seq, d)
```

```python
import functools

import jax
import jax.numpy as jnp
from jax import lax
from jax.experimental import pallas as pl
from jax.experimental.pallas import tpu as pltpu

EPS = 1e-5
HEAD_DIM = 128
CHUNK = 128
CONV_K = 3
LANES = 128
HALO = 8

_VMEM_LIMIT = 60000 * 1024

_ARB2 = ("arbitrary", "arbitrary")


def _params(sem):
    return pltpu.CompilerParams(dimension_semantics=sem, vmem_limit_bytes=_VMEM_LIMIT)


def _bdot(a, b):
    return jnp.dot(a.astype(jnp.bfloat16), b.astype(jnp.bfloat16),
                   preferred_element_type=jnp.float32)


def _lane_group_sum(sq):
    acc = sq[:, 0:LANES]
    for c in range(1, sq.shape[1] // LANES):
        acc = acc + sq[:, c * LANES:(c + 1) * LANES]
    return acc


def _rep_lanes(v, width):
    return jnp.concatenate([v] * (width // LANES), axis=1)


def _zero_row_sums_at_start(acc_ref):
    @pl.when((pl.program_id(0) == 0) & (pl.program_id(1) == 0))
    def _():
        acc_ref[...] = jnp.zeros(acc_ref.shape, acc_ref.dtype)


def _inv_rms_spec(tm, nj):
    return pl.BlockSpec((tm, LANES), lambda j, i: (jnp.where(j == nj - 1, i, 0), 0))


def _accumulate_inv_rms(acc_ref, inv_ref, row0, sq, width):
    rc = sq.shape[0]
    tm = inv_ref.shape[0]
    rows = pl.ds(pl.multiple_of(pl.program_id(1) * tm + row0, rc), rc)
    ssq = acc_ref[rows, :] + sq
    acc_ref[rows, :] = ssq
    total = jnp.sum(ssq, axis=-1, keepdims=True)
    inv_ref[row0:row0 + rc, :] = jnp.broadcast_to(lax.rsqrt(total / width + EPS),
                                                  (rc, inv_ref.shape[1]))


def _rmsnorm_kernel(x_ref, g_ref, o_ref):
    x = x_ref[...]
    ms = jnp.mean(x * x, axis=-1, keepdims=True)
    o_ref[...] = (x * lax.rsqrt(ms + EPS) * g_ref[...]).astype(o_ref.dtype)


def _rmsnorm(x, g, out_dtype, tr=512):
    t, d = x.shape
    return pl.pallas_call(
        _rmsnorm_kernel,
        grid=(t // tr,),
        in_specs=[pl.BlockSpec((tr, d), lambda i: (i, 0)),
                  pl.BlockSpec((1, d), lambda i: (0, 0))],
        out_specs=pl.BlockSpec((tr, d), lambda i: (i, 0)),
        out_shape=jax.ShapeDtypeStruct((t, d), out_dtype),
        compiler_params=_params(("arbitrary",)),
        name="rmsnorm_cast",
    )(x, g.reshape(1, d))


def _cast_weights_once(w_refs, wbf_ref):
    @pl.when(pl.program_id(1) == 0)
    def _():
        for n, w_ref in enumerate(w_refs):
            wbf_ref[n] = w_ref[...].astype(wbf_ref.dtype)


def _conv_kernel(tiles_per_seq, rc, width, x_ref, wb_ref, wc_ref, wh_ref, cw_ref, g_ref,
                 y_ref, inv_ref, wbf_ref, ext_ref, acc_ref):
    i = pl.program_id(1)
    tm, tn = y_ref.shape
    _cast_weights_once((wb_ref, wc_ref, wh_ref), wbf_ref)

    @pl.when(i % tiles_per_seq == 0)
    def _():
        ext_ref[0:HALO, :] = jnp.zeros((HALO, tn), jnp.float32)

    _zero_row_sums_at_start(acc_ref)
    for r in range(tm // rc):
        rs = slice(r * rc, (r + 1) * rc)
        xb = x_ref[rs, :]
        bg = _bdot(xb, wbf_ref[0])
        ch = _bdot(xb, wbf_ref[1]) * _bdot(xb, wbf_ref[2])
        base = HALO + r * rc
        ext_ref[base:base + rc, :] = ch
        conv = cw_ref[CONV_K - 1:CONV_K, :] * ch
        for k in range(CONV_K - 1):
            shift = CONV_K - 1 - k
            conv = conv + cw_ref[k:k + 1, :] * ext_ref[base - shift:base - shift + rc, :]
        ya = bg * conv
        y_ref[rs, :] = (ya * g_ref[...]).astype(y_ref.dtype)
        _accumulate_inv_rms(acc_ref, inv_ref, r * rc, _lane_group_sum(ya * ya), width)
    ext_ref[0:HALO, :] = ext_ref[tm:tm + HALO, :]


def _conv_mixer(xn, w_in, conv_w, g_a, seq, cw, tm=1024, tn=256, rc=256):
    t, d = xn.shape
    nj = cw // tn
    kern = functools.partial(_conv_kernel, seq // tm, rc, cw)
    wspec = lambda off: pl.BlockSpec((d, tn), lambda j, i, off=off: (0, j + off))
    return pl.pallas_call(
        kern,
        grid=(nj, t // tm),
        in_specs=[pl.BlockSpec((tm, d), lambda j, i: (i, 0)),
                  wspec(0), wspec(nj), wspec(2 * nj),
                  pl.BlockSpec((CONV_K, tn), lambda j, i: (0, j)),
                  pl.BlockSpec((1, tn), lambda j, i: (0, j))],
        out_specs=[pl.BlockSpec((tm, tn), lambda j, i: (i, j)), _inv_rms_spec(tm, nj)],
        out_shape=[jax.ShapeDtypeStruct((t, cw), jnp.bfloat16),
                   jax.ShapeDtypeStruct((t, LANES), jnp.float32)],
        scratch_shapes=[pltpu.VMEM((3, d, tn), jnp.bfloat16),
                        pltpu.VMEM((HALO + tm, tn), jnp.float32),
                        pltpu.VMEM((t, LANES), jnp.float32)],
        compiler_params=_params(_ARB2),
        name="conv_mixer",
    )(xn, w_in, w_in, w_in, conv_w, g_a.reshape(1, cw))


def _gmlp_kernel(rc, width, x_ref, wu_ref, wv_ref, sw_ref, sb_ref, g_ref, wo_ref,
                 y_ref, inv_ref, wob_ref, wbf_ref, acc_ref):
    tm, tn = y_ref.shape
    wob_ref[...] = wo_ref[...].astype(wob_ref.dtype)
    _cast_weights_once((wu_ref, wv_ref), wbf_ref)
    row = lax.broadcasted_iota(jnp.int32, (CHUNK, CHUNK), 0)
    col = lax.broadcasted_iota(jnp.int32, (CHUNK, CHUNK), 1)
    causal = col <= row
    heads = tn // HEAD_DIM
    ws = [jnp.where(causal, sw_ref[h], 0.0).astype(jnp.bfloat16) for h in range(heads)]
    n_chunks = rc // CHUNK
    n_rc = tm // rc
    _zero_row_sums_at_start(acc_ref)

    def uv_dots(r):
        xb = x_ref[r * rc:(r + 1) * rc, :]
        return _bdot(xb, wbf_ref[0]), _bdot(xb, wbf_ref[1])

    uv = uv_dots(0)
    for r in range(n_rc):
        rs = slice(r * rc, (r + 1) * rc)
        u, v = uv
        if r + 1 < n_rc:
            uv = uv_dots(r + 1)
        gu = jax.nn.gelu(u)
        gv = jax.nn.gelu(v).astype(jnp.bfloat16)
        gate = []
        for h in range(heads):
            hs = slice(h * HEAD_DIM, (h + 1) * HEAD_DIM)
            vcat = jnp.concatenate(
                [gv[c * CHUNK:(c + 1) * CHUNK, hs] for c in range(n_chunks)], axis=1)
            gate.append(_bdot(ws[h], vcat))
        s = jnp.concatenate(
            [jnp.concatenate([gate[h][:, c * CHUNK:(c + 1) * CHUNK] + sb_ref[h]
                              for h in range(heads)], axis=1)
             for c in range(n_chunks)], axis=0)
        yb = gu * s
        y_ref[rs, :] = (yb * g_ref[...]).astype(y_ref.dtype)
        _accumulate_inv_rms(acc_ref, inv_ref, r * rc, _lane_group_sum(yb * yb), width)


def _slab_spec(rows, cols, steps_inner, n_steps):
    return pl.BlockSpec((rows // n_steps, cols), lambda j, i: (j * steps_inner + i, 0))


def _gmlp_mixer(xn, w_in, spatial_w, spatial_b, g_b, col0, gw, w_out, tm=1024, tn=256,
                rc=256):
    t, d = xn.shape
    nj, ni = gw // tn, t // tm
    hpt = tn // HEAD_DIM
    off_u = col0 // tn
    wo_spec = _slab_spec(w_out.shape[0], w_out.shape[1], ni, nj * ni)
    sb = jnp.broadcast_to(spatial_b[:, :, None], spatial_b.shape + (HEAD_DIM,))
    kern = functools.partial(_gmlp_kernel, rc, gw)
    return pl.pallas_call(
        kern,
        grid=(nj, ni),
        in_specs=[pl.BlockSpec((tm, d), lambda j, i: (i, 0)),
                  pl.BlockSpec((d, tn), lambda j, i: (0, j + off_u)),
                  pl.BlockSpec((d, tn), lambda j, i: (0, j + off_u + nj)),
                  pl.BlockSpec((hpt, CHUNK, CHUNK), lambda j, i: (j, 0, 0)),
                  pl.BlockSpec((hpt, CHUNK, HEAD_DIM), lambda j, i: (j, 0, 0)),
                  pl.BlockSpec((1, tn), lambda j, i: (0, j)),
                  wo_spec],
        out_specs=[pl.BlockSpec((tm, tn), lambda j, i: (i, j)), _inv_rms_spec(tm, nj),
                   wo_spec],
        out_shape=[jax.ShapeDtypeStruct((t, gw), jnp.bfloat16),
                   jax.ShapeDtypeStruct((t, LANES), jnp.float32),
                   jax.ShapeDtypeStruct(w_out.shape, jnp.bfloat16)],
        scratch_shapes=[pltpu.VMEM((2, d, tn), jnp.bfloat16),
                        pltpu.VMEM((t, LANES), jnp.float32)],
        compiler_params=_params(_ARB2),
        name="gmlp_mixer",
    )(xn, w_in, w_in, spatial_w, sb, g_b.reshape(1, gw), w_out)


def _out_proj_kernel(rc, width, ya_ref, yb_ref, wa_ref, wb_ref, ia_ref, ib_ref, x_ref, g_ref,
                     h_ref, xg_ref, inv_ref, acc_ref):
    tm, tn = h_ref.shape
    _zero_row_sums_at_start(acc_ref)
    for r in range(tm // rc):
        rs = slice(r * rc, (r + 1) * rc)
        pa = _bdot(ya_ref[rs, :], wa_ref[...]) * _rep_lanes(ia_ref[rs, :], tn)
        pb = _bdot(yb_ref[rs, :], wb_ref[...]) * _rep_lanes(ib_ref[rs, :], tn)
        h = x_ref[rs, :] + pa + pb
        h_ref[rs, :] = h
        xg_ref[rs, :] = (h * g_ref[...]).astype(xg_ref.dtype)
        _accumulate_inv_rms(acc_ref, inv_ref, r * rc, _lane_group_sum(h * h), width)


def _out_proj(ya, yb, w_out, inv_a, inv_b, x, g_mlp, tm=512, tn=1024, rc=256):
    t, cw = ya.shape
    gw = yb.shape[1]
    d = w_out.shape[1]
    assert cw == gw
    kern = functools.partial(_out_proj_kernel, rc, d)
    row_spec = pl.BlockSpec((tm, LANES), lambda j, i: (i, 0))
    tile_spec = pl.BlockSpec((tm, tn), lambda j, i: (i, j))
    return pl.pallas_call(
        kern,
        grid=(d // tn, t // tm),
        in_specs=[pl.BlockSpec((tm, cw), lambda j, i: (i, 0)),
                  pl.BlockSpec((tm, gw), lambda j, i: (i, 0)),
                  pl.BlockSpec((cw, tn), lambda j, i: (0, j)),
                  pl.BlockSpec((gw, tn), lambda j, i: (1, j)),
                  row_spec, row_spec, tile_spec,
                  pl.BlockSpec((1, tn), lambda j, i: (0, j))],
        out_specs=[tile_spec, tile_spec, _inv_rms_spec(tm, d // tn)],
        out_shape=[jax.ShapeDtypeStruct((t, d), jnp.float32),
                   jax.ShapeDtypeStruct((t, d), jnp.bfloat16),
                   jax.ShapeDtypeStruct((t, LANES), jnp.float32)],
        scratch_shapes=[pltpu.VMEM((t, LANES), jnp.float32)],
        compiler_params=_params(_ARB2),
        name="out_proj",
    )(ya, yb, w_out, w_out, inv_a, inv_b, x, g_mlp.reshape(1, d))


def _mlp_up_kernel(rc, x_ref, w_ref, wd_ref, r_ref, wdb_ref, wbf_ref):
    wdb_ref[...] = wd_ref[...].astype(wdb_ref.dtype)
    _cast_weights_once((w_ref,), wbf_ref)
    for r in range(r_ref.shape[0] // rc):
        rs = slice(r * rc, (r + 1) * rc)
        z = jnp.maximum(_bdot(x_ref[rs, :], wbf_ref[0]), 0.0)
        r_ref[rs, :] = (z * z).astype(r_ref.dtype)


def _mlp_up(xg, w_up, w_down, tm=512, tn=1024, rc=256):
    t, d = xg.shape
    f = w_up.shape[1]
    nj, ni = f // tn, t // tm
    kern = functools.partial(_mlp_up_kernel, rc)
    slab_spec = _slab_spec(f, d, ni, nj * ni)
    return pl.pallas_call(
        kern,
        grid=(nj, ni),
        in_specs=[pl.BlockSpec((tm, d), lambda j, i: (i, 0)),
                  pl.BlockSpec((d, tn), lambda j, i: (0, j)),
                  slab_spec],
        out_specs=[pl.BlockSpec((tm, tn), lambda j, i: (i, j)), slab_spec],
        out_shape=[jax.ShapeDtypeStruct((t, f), jnp.bfloat16),
                   jax.ShapeDtypeStruct(w_down.shape, jnp.bfloat16)],
        scratch_shapes=[pltpu.VMEM((1, d, tn), jnp.bfloat16)],
        compiler_params=_params(_ARB2),
        name="mlp_up",
    )(xg, w_up, w_down)


def _mlp_down_kernel(r_ref, w_ref, h_ref, inv_ref, o_ref):
    k = pl.program_id(2)

    @pl.when(k == 0)
    def _():
        o_ref[...] = jnp.zeros(o_ref.shape, o_ref.dtype)

    o_ref[...] += _bdot(r_ref[...], w_ref[...])

    @pl.when(k == pl.num_programs(2) - 1)
    def _():
        inv = _rep_lanes(inv_ref[...], o_ref.shape[1])
        o_ref[...] = h_ref[...] + inv * inv * o_ref[...]


def _mlp_down(r, w_down, h, inv, tm=1024, tn=1024, tk=4096):
    t, f = r.shape
    d = w_down.shape[1]
    return pl.pallas_call(
        _mlp_down_kernel,
        grid=(d // tn, t // tm, f // tk),
        in_specs=[pl.BlockSpec((tm, tk), lambda j, i, k: (i, k)),
                  pl.BlockSpec((tk, tn), lambda j, i, k: (k, j)),
                  pl.BlockSpec((tm, tn), lambda j, i, k: (i, j)),
                  pl.BlockSpec((tm, LANES), lambda j, i, k: (i, 0))],
        out_specs=pl.BlockSpec((tm, tn), lambda j, i, k: (i, j)),
        out_shape=jax.ShapeDtypeStruct((t, d), jnp.float32),
        compiler_params=_params(("arbitrary", "arbitrary", "arbitrary")),
        name="mlp_down",
    )(r, w_down, h, inv)


def kernel(x, mix_norm_g, w_in, conv_w, spatial_w, spatial_b, conv_out_norm_g,
           gmlp_out_norm_g, w_out, mlp_norm_g, w_up, w_down, final_norm_g):
    bsz, seq, d = x.shape
    depth = w_in.shape[0]
    cw = conv_w.shape[2]
    gw = spatial_w.shape[1] * HEAD_DIM
    assert seq % CHUNK == 0 and w_in.shape[2] == 3 * cw + 2 * gw
    h = x.reshape(bsz * seq, d)
    for l in range(depth):
        xn = _rmsnorm(h, mix_norm_g[l], jnp.bfloat16)
        ya, inv_a = _conv_mixer(xn, w_in[l], conv_w[l], conv_out_norm_g[l], seq, cw)
        yb, inv_b, w_out_bf16 = _gmlp_mixer(xn, w_in[l], spatial_w[l], spatial_b[l],
                                            gmlp_out_norm_g[l], 3 * cw, gw, w_out[l])
        h, xg, inv_h = _out_proj(ya, yb, w_out_bf16, inv_a, inv_b, h, mlp_norm_g[l])
        r, w_down_bf16 = _mlp_up(xg, w_up[l], w_down[l])
        h = _mlp_down(r, w_down_bf16, h, inv_h)
    out = _rmsnorm(h, final_norm_g, x.dtype)
    return out.reshape(bsz, seq, d)
```

```python
import functools

import jax
import jax.numpy as jnp
from jax import lax
from jax.experimental import pallas as pl
from jax.experimental.pallas import tpu as pltpu

EPS = 1e-5
HEAD_DIM = 128
CHUNK = 128
CONV_K = 3
LANES = 128
HALO = 8

_VMEM_LIMIT = 60000 * 1024

_ARB2 = ("arbitrary", "arbitrary")


def _params(sem):
    return pltpu.CompilerParams(dimension_semantics=sem, vmem_limit_bytes=_VMEM_LIMIT)


def _bdot(a, b):
    return jnp.dot(a.astype(jnp.bfloat16), b.astype(jnp.bfloat16),
                   preferred_element_type=jnp.float32)


def _lane_group_sum(sq):
    acc = sq[:, 0:LANES]
    for c in range(1, sq.shape[1] // LANES):
        acc = acc + sq[:, c * LANES:(c + 1) * LANES]
    return acc


def _rep_lanes(v, width):
    return jnp.concatenate([v] * (width // LANES), axis=1)


def _zero_row_sums_at_start(acc_ref):
    @pl.when((pl.program_id(0) == 0) & (pl.program_id(1) == 0))
    def _():
        acc_ref[...] = jnp.zeros(acc_ref.shape, acc_ref.dtype)


def _inv_rms_spec(tm, nj):
    return pl.BlockSpec((tm, LANES), lambda j, i: (jnp.where(j == nj - 1, i, 0), 0))


def _accumulate_inv_rms(acc_ref, inv_ref, row0, sq, width):
    rc = sq.shape[0]
    tm = inv_ref.shape[0]
    rows = pl.ds(pl.multiple_of(pl.program_id(1) * tm + row0, rc), rc)
    ssq = acc_ref[rows, :] + sq
    acc_ref[rows, :] = ssq
    total = jnp.sum(ssq, axis=-1, keepdims=True)
    inv_ref[row0:row0 + rc, :] = jnp.broadcast_to(lax.rsqrt(total / width + EPS),
                                                  (rc, inv_ref.shape[1]))


def _rmsnorm_kernel(x_ref, g_ref, o_ref):
    x = x_ref[...]
    ms = jnp.mean(x * x, axis=-1, keepdims=True)
    o_ref[...] = (x * lax.rsqrt(ms + EPS) * g_ref[...]).astype(o_ref.dtype)


def _rmsnorm(x, g, out_dtype, tr=512):
    t, d = x.shape
    return pl.pallas_call(
        _rmsnorm_kernel,
        grid=(t // tr,),
        in_specs=[pl.BlockSpec((tr, d), lambda i: (i, 0)),
                  pl.BlockSpec((1, d), lambda i: (0, 0))],
        out_specs=pl.BlockSpec((tr, d), lambda i: (i, 0)),
        out_shape=jax.ShapeDtypeStruct((t, d), out_dtype),
        compiler_params=_params(("arbitrary",)),
        name="rmsnorm_cast",
    )(x, g.reshape(1, d))


def _cast_weights_once(w_refs, wbf_ref):
    @pl.when(pl.program_id(1) == 0)
    def _():
        for n, w_ref in enumerate(w_refs):
            wbf_ref[n] = w_ref[...].astype(wbf_ref.dtype)


def _conv_kernel(tiles_per_seq, rc, width, x_ref, wb_ref, wc_ref, wh_ref, cw_ref, g_ref,
                 y_ref, inv_ref, wbf_ref, ext_ref, acc_ref):
    i = pl.program_id(1)
    tm, tn = y_ref.shape
    _cast_weights_once((wb_ref, wc_ref, wh_ref), wbf_ref)

    @pl.when(i % tiles_per_seq == 0)
    def _():
        ext_ref[0:HALO, :] = jnp.zeros((HALO, tn), jnp.float32)

    _zero_row_sums_at_start(acc_ref)
    for r in range(tm // rc):
        rs = slice(r * rc, (r + 1) * rc)
        xb = x_ref[rs, :]
        bg = _bdot(xb, wbf_ref[0])
        ch = _bdot(xb, wbf_ref[1]) * _bdot(xb, wbf_ref[2])
        base = HALO + r * rc
        ext_ref[base:base + rc, :] = ch
        conv = cw_ref[CONV_K - 1:CONV_K, :] * ch
        for k in range(CONV_K - 1):
            shift = CONV_K - 1 - k
            conv = conv + cw_ref[k:k + 1, :] * ext_ref[base - shift:base - shift + rc, :]
        ya = bg * conv
        y_ref[rs, :] = (ya * g_ref[...]).astype(y_ref.dtype)
        _accumulate_inv_rms(acc_ref, inv_ref, r * rc, _lane_group_sum(ya * ya), width)
    ext_ref[0:HALO, :] = ext_ref[tm:tm + HALO, :]


def _conv_mixer(xn, w_in, conv_w, g_a, seq, cw, tm=1024, tn=256, rc=256):
    t, d = xn.shape
    nj = cw // tn
    kern = functools.partial(_conv_kernel, seq // tm, rc, cw)
    wspec = lambda off: pl.BlockSpec((d, tn), lambda j, i, off=off: (0, j + off))
    return pl.pallas_call(
        kern,
        grid=(nj, t // tm),
        in_specs=[pl.BlockSpec((tm, d), lambda j, i: (i, 0)),
                  wspec(0), wspec(nj), wspec(2 * nj),
                  pl.BlockSpec((CONV_K, tn), lambda j, i: (0, j)),
                  pl.BlockSpec((1, tn), lambda j, i: (0, j))],
        out_specs=[pl.BlockSpec((tm, tn), lambda j, i: (i, j)), _inv_rms_spec(tm, nj)],
        out_shape=[jax.ShapeDtypeStruct((t, cw), jnp.bfloat16),
                   jax.ShapeDtypeStruct((t, LANES), jnp.float32)],
        scratch_shapes=[pltpu.VMEM((3, d, tn), jnp.bfloat16),
                        pltpu.VMEM((HALO + tm, tn), jnp.float32),
                        pltpu.VMEM((t, LANES), jnp.float32)],
        compiler_params=_params(_ARB2),
        name="conv_mixer",
    )(xn, w_in, w_in, w_in, conv_w, g_a.reshape(1, cw))


def _gmlp_kernel(rc, width, x_ref, wu_ref, wv_ref, sw_ref, sb_ref, g_ref, wo_ref,
                 y_ref, inv_ref, wob_ref, wbf_ref, acc_ref):
    tm, tn = y_ref.shape
    wob_ref[...] = wo_ref[...].astype(wob_ref.dtype)
    _cast_weights_once((wu_ref, wv_ref), wbf_ref)
    row = lax.broadcasted_iota(jnp.int32, (CHUNK, CHUNK), 0)
    col = lax.broadcasted_iota(jnp.int32, (CHUNK, CHUNK), 1)
    causal = col <= row
    heads = tn // HEAD_DIM
    ws = [jnp.where(causal, sw_ref[h], 0.0).astype(jnp.bfloat16) for h in range(heads)]
    n_chunks = rc // CHUNK
    n_rc = tm // rc
    _zero_row_sums_at_start(acc_ref)

    def uv_dots(r):
        xb = x_ref[r * rc:(r + 1) * rc, :]
        return _bdot(xb, wbf_ref[0]), _bdot(xb, wbf_ref[1])

    uv = uv_dots(0)
    for r in range(n_rc):
        rs = slice(r * rc, (r + 1) * rc)
        u, v = uv
        if r + 1 < n_rc:
            uv = uv_dots(r + 1)
        gu = jax.nn.gelu(u)
        gv = jax.nn.gelu(v).astype(jnp.bfloat16)
        gate = []
        for h in range(heads):
            hs = slice(h * HEAD_DIM, (h + 1) * HEAD_DIM)
            vcat = jnp.concatenate(
                [gv[c * CHUNK:(c + 1) * CHUNK, hs] for c in range(n_chunks)], axis=1)
            gate.append(_bdot(ws[h], vcat))
        s = jnp.concatenate(
            [jnp.concatenate([gate[h][:, c * CHUNK:(c + 1) * CHUNK] + sb_ref[h]
                              for h in range(heads)], axis=1)
             for c in range(n_chunks)], axis=0)
        yb = gu * s
        y_ref[rs, :] = (yb * g_ref[...]).astype(y_ref.dtype)
        _accumulate_inv_rms(acc_ref, inv_ref, r * rc, _lane_group_sum(yb * yb), width)


def _slab_spec(rows, cols, steps_inner, n_steps):
    return pl.BlockSpec((rows // n_steps, cols), lambda j, i: (j * steps_inner + i, 0))


def _gmlp_mixer(xn, w_in, spatial_w, spatial_b, g_b, col0, gw, w_out, tm=1024, tn=256,
                rc=256):
    t, d = xn.shape
    nj, ni = gw // tn, t // tm
    hpt = tn // HEAD_DIM
    off_u = col0 // tn
    wo_spec = _slab_spec(w_out.shape[0], w_out.shape[1], ni, nj * ni)
    sb = jnp.broadcast_to(spatial_b[:, :, None], spatial_b.shape + (HEAD_DIM,))
    kern = functools.partial(_gmlp_kernel, rc, gw)
    return pl.pallas_call(
        kern,
        grid=(nj, ni),
        in_specs=[pl.BlockSpec((tm, d), lambda j, i: (i, 0)),
                  pl.BlockSpec((d, tn), lambda j, i: (0, j + off_u)),
                  pl.BlockSpec((d, tn), lambda j, i: (0, j + off_u + nj)),
                  pl.BlockSpec((hpt, CHUNK, CHUNK), lambda j, i: (j, 0, 0)),
                  pl.BlockSpec((hpt, CHUNK, HEAD_DIM), lambda j, i: (j, 0, 0)),
                  pl.BlockSpec((1, tn), lambda j, i: (0, j)),
                  wo_spec],
        out_specs=[pl.BlockSpec((tm, tn), lambda j, i: (i, j)), _inv_rms_spec(tm, nj),
                   wo_spec],
        out_shape=[jax.ShapeDtypeStruct((t, gw), jnp.bfloat16),
                   jax.ShapeDtypeStruct((t, LANES), jnp.float32),
                   jax.ShapeDtypeStruct(w_out.shape, jnp.bfloat16)],
        scratch_shapes=[pltpu.VMEM((2, d, tn), jnp.bfloat16),
                        pltpu.VMEM((t, LANES), jnp.float32)],
        compiler_params=_params(_ARB2),
        name="gmlp_mixer",
    )(xn, w_in, w_in, spatial_w, sb, g_b.reshape(1, gw), w_out)


def _out_proj_kernel(rc, width, ya_ref, yb_ref, wa_ref, wb_ref, ia_ref, ib_ref, x_ref, g_ref,
                     h_ref, xg_ref, inv_ref, acc_ref):
    tm, tn = h_ref.shape
    _zero_row_sums_at_start(acc_ref)
    for r in range(tm // rc):
        rs = slice(r * rc, (r + 1) * rc)
        pa = _bdot(ya_ref[rs, :], wa_ref[...]) * _rep_lanes(ia_ref[rs, :], tn)
        pb = _bdot(yb_ref[rs, :], wb_ref[...]) * _rep_lanes(ib_ref[rs, :], tn)
        h = x_ref[rs, :] + pa + pb
        h_ref[rs, :] = h
        xg_ref[rs, :] = (h * g_ref[...]).astype(xg_ref.dtype)
        _accumulate_inv_rms(acc_ref, inv_ref, r * rc, _lane_group_sum(h * h), width)


def _out_proj(ya, yb, w_out, inv_a, inv_b, x, g_mlp, tm=512, tn=1024, rc=256):
    t, cw = ya.shape
    gw = yb.shape[1]
    d = w_out.shape[1]
    assert cw == gw
    kern = functools.partial(_out_proj_kernel, rc, d)
    row_spec = pl.BlockSpec((tm, LANES), lambda j, i: (i, 0))
    tile_spec = pl.BlockSpec((tm, tn), lambda j, i: (i, j))
    return pl.pallas_call(
        kern,
        grid=(d // tn, t // tm),
        in_specs=[pl.BlockSpec((tm, cw), lambda j, i: (i, 0)),
                  pl.BlockSpec((tm, gw), lambda j, i: (i, 0)),
                  pl.BlockSpec((cw, tn), lambda j, i: (0, j)),
                  pl.BlockSpec((gw, tn), lambda j, i: (1, j)),
                  row_spec, row_spec, tile_spec,
                  pl.BlockSpec((1, tn), lambda j, i: (0, j))],
        out_specs=[tile_spec, tile_spec, _inv_rms_spec(tm, d // tn)],
        out_shape=[jax.ShapeDtypeStruct((t, d), jnp.float32),
                   jax.ShapeDtypeStruct((t, d), jnp.bfloat16),
                   jax.ShapeDtypeStruct((t, LANES), jnp.float32)],
        scratch_shapes=[pltpu.VMEM((t, LANES), jnp.float32)],
        compiler_params=_params(_ARB2),
        name="out_proj",
    )(ya, yb, w_out, w_out, inv_a, inv_b, x, g_mlp.reshape(1, d))


def _mlp_up_kernel(rc, x_ref, wc_ref, wd_ref, r_ref, wdb_ref, wbf_a, wbf_b):
    p = pl.program_id(0)
    kc = wc_ref.shape[0]
    chunk_rows = pl.ds(pl.multiple_of(pl.program_id(1) * kc, kc), kc)
    wdb_ref[...] = wd_ref[...].astype(wdb_ref.dtype)

    def step(cur_ref, nxt_ref):
        nxt_ref[chunk_rows, :] = wc_ref[...].astype(nxt_ref.dtype)
        if cur_ref is None:
            return
        for r in range(r_ref.shape[0] // rc):
            rs = slice(r * rc, (r + 1) * rc)
            z = jnp.maximum(_bdot(x_ref[rs, :], cur_ref[...]), 0.0)
            r_ref[rs, :] = (z * z).astype(r_ref.dtype)

    pl.when(p == 0)(lambda: step(None, wbf_a))
    pl.when((p > 0) & (p % 2 == 1))(lambda: step(wbf_a, wbf_b))
    pl.when((p > 0) & (p % 2 == 0))(lambda: step(wbf_b, wbf_a))


def _mlp_up(xg, w_up, w_down, tm=1024, tn=1024, rc=1024):
    t, d = xg.shape
    f = w_up.shape[1]
    nj, ni = f // tn, t // tm
    kern = functools.partial(_mlp_up_kernel, rc)
    row_tile = lambda p, i: jnp.where(p == 0, 0, i)
    slab_spec = pl.BlockSpec((f // (nj * ni), d),
                             lambda p, i: (jnp.maximum(p - 1, 0) * ni + row_tile(p, i), 0))
    return pl.pallas_call(
        kern,
        grid=(nj + 1, ni),
        in_specs=[pl.BlockSpec((tm, d), lambda p, i: (row_tile(p, i), 0)),
                  pl.BlockSpec((d // ni, tn), lambda p, i: (i, jnp.minimum(p, nj - 1))),
                  slab_spec],
        out_specs=[pl.BlockSpec((tm, tn),
                                lambda p, i: (row_tile(p, i), jnp.maximum(p - 1, 0))),
                   slab_spec],
        out_shape=[jax.ShapeDtypeStruct((t, f), jnp.bfloat16),
                   jax.ShapeDtypeStruct(w_down.shape, jnp.bfloat16)],
        scratch_shapes=[pltpu.VMEM((d, tn), jnp.bfloat16),
                        pltpu.VMEM((d, tn), jnp.bfloat16)],
        compiler_params=_params(_ARB2),
        name="mlp_up",
    )(xg, w_up, w_down)


def _mlp_down_kernel(r_ref, w_ref, h_ref, inv_ref, o_ref):
    k = pl.program_id(2)
    last = pl.num_programs(2) - 1

    @pl.when(k == 0)
    def _():
        o_ref[...] = _bdot(r_ref[...], w_ref[...])

    @pl.when((k > 0) & (k < last))
    def _():
        o_ref[...] += _bdot(r_ref[...], w_ref[...])

    @pl.when(k == last)
    def _():
        inv = _rep_lanes(inv_ref[...], o_ref.shape[1])
        o_ref[...] = h_ref[...] + inv * inv * (o_ref[...] + _bdot(r_ref[...], w_ref[...]))


def _mlp_down(r, w_down, h, inv, tm=1024, tn=1024, tk=4096):
    t, f = r.shape
    d = w_down.shape[1]
    assert f // tk >= 2
    return pl.pallas_call(
        _mlp_down_kernel,
        grid=(d // tn, t // tm, f // tk),
        in_specs=[pl.BlockSpec((tm, tk), lambda j, i, k: (i, k)),
                  pl.BlockSpec((tk, tn), lambda j, i, k: (k, j)),
                  pl.BlockSpec((tm, tn), lambda j, i, k: (i, j)),
                  pl.BlockSpec((tm, LANES), lambda j, i, k: (i, 0))],
        out_specs=pl.BlockSpec((tm, tn), lambda j, i, k: (i, j)),
        out_shape=jax.ShapeDtypeStruct((t, d), jnp.float32),
        compiler_params=_params(("arbitrary", "arbitrary", "arbitrary")),
        name="mlp_down",
    )(r, w_down, h, inv)


def kernel(x, mix_norm_g, w_in, conv_w, spatial_w, spatial_b, conv_out_norm_g,
           gmlp_out_norm_g, w_out, mlp_norm_g, w_up, w_down, final_norm_g):
    bsz, seq, d = x.shape
    depth = w_in.shape[0]
    cw = conv_w.shape[2]
    gw = spatial_w.shape[1] * HEAD_DIM
    assert seq % CHUNK == 0 and w_in.shape[2] == 3 * cw + 2 * gw
    h = x.reshape(bsz * seq, d)
    for l in range(depth):
        xn = _rmsnorm(h, mix_norm_g[l], jnp.bfloat16)
        ya, inv_a = _conv_mixer(xn, w_in[l], conv_w[l], conv_out_norm_g[l], seq, cw)
        yb, inv_b, w_out_bf16 = _gmlp_mixer(xn, w_in[l], spatial_w[l], spatial_b[l],
                                            gmlp_out_norm_g[l], 3 * cw, gw, w_out[l])
        h, xg, inv_h = _out_proj(ya, yb, w_out_bf16, inv_a, inv_b, h, mlp_norm_g[l])
        r, w_down_bf16 = _mlp_up(xg, w_up[l], w_down[l])
        h = _mlp_down(r, w_down_bf16, h, inv_h)
    out = _rmsnorm(h, final_norm_g, x.dtype)
    return out.reshape(bsz, seq, d)
```

```python
import functools
import math

import jax
import jax.numpy as jnp
from jax import lax
from jax.experimental import pallas as pl
from jax.experimental.pallas import tpu as pltpu

EPS = 1e-5
HEAD_DIM = 128
CHUNK = 128
CONV_K = 3
LANES = 128
HALO = 8

_VMEM_LIMIT = 60000 * 1024

_ARB2 = ("arbitrary", "arbitrary")


def _params(sem):
    return pltpu.CompilerParams(dimension_semantics=sem, vmem_limit_bytes=_VMEM_LIMIT)


def _bdot(a, b):
    return jnp.dot(a.astype(jnp.bfloat16), b.astype(jnp.bfloat16),
                   preferred_element_type=jnp.float32)


def _lane_group_sum(sq):
    acc = sq[:, 0:LANES]
    for c in range(1, sq.shape[1] // LANES):
        acc = acc + sq[:, c * LANES:(c + 1) * LANES]
    return acc


def _rep_lanes(v, width):
    return jnp.concatenate([v] * (width // LANES), axis=1)


def _zero_row_sums_at_start(acc_ref):
    @pl.when((pl.program_id(0) == 0) & (pl.program_id(1) == 0))
    def _():
        acc_ref[...] = jnp.zeros(acc_ref.shape, acc_ref.dtype)


def _inv_rms_spec(tm, last):
    return pl.BlockSpec((tm, LANES), lambda j, i: (jnp.where(j == last, i, 0), 0))


def _chunks(sizes):
    out, start = [], 0
    for size in sizes:
        out.append((start, size))
        start += size
    return out


def _pass_row_tile(p, i):
    return jnp.where(p == 0, 0, i)


def _pass_col_tile(p):
    return jnp.maximum(p - 1, 0)


def _pass_slab_spec(rows, cols, nj, ni):
    return pl.BlockSpec((rows // (nj * ni), cols),
                        lambda p, i: (_pass_col_tile(p) * ni + _pass_row_tile(p, i), 0))


def _run_streamed_pass(step, wbf_a, wbf_b):
    p = pl.program_id(0)
    pl.when(p == 0)(lambda: step(None, wbf_a))
    pl.when((p > 0) & (p % 2 == 1))(lambda: step(wbf_a, wbf_b))
    pl.when((p > 0) & (p % 2 == 0))(lambda: step(wbf_b, wbf_a))


def _accumulate_inv_rms(acc_ref, inv_ref, row0, sq, width):
    rc = sq.shape[0]
    tm = inv_ref.shape[0]
    rows = pl.ds(pl.multiple_of(pl.program_id(1) * tm + row0, math.gcd(tm, row0)), rc)
    ssq = acc_ref[rows, :] + sq
    acc_ref[rows, :] = ssq
    total = jnp.sum(ssq, axis=-1, keepdims=True)
    inv_ref[row0:row0 + rc, :] = jnp.broadcast_to(lax.rsqrt(total / width + EPS),
                                                  (rc, inv_ref.shape[1]))


def _rmsnorm_kernel(x_ref, g_ref, o_ref):
    x = x_ref[...]
    ms = jnp.mean(x * x, axis=-1, keepdims=True)
    o_ref[...] = (x * lax.rsqrt(ms + EPS) * g_ref[...]).astype(o_ref.dtype)


def _rmsnorm(x, g, out_dtype, tr=512):
    t, d = x.shape
    return pl.pallas_call(
        _rmsnorm_kernel,
        grid=(t // tr,),
        in_specs=[pl.BlockSpec((tr, d), lambda i: (i, 0)),
                  pl.BlockSpec((1, d), lambda i: (0, 0))],
        out_specs=pl.BlockSpec((tr, d), lambda i: (i, 0)),
        out_shape=jax.ShapeDtypeStruct((t, d), out_dtype),
        compiler_params=_params(("arbitrary",)),
        name="rmsnorm_cast",
    )(x, g.reshape(1, d))


def _cast_weights_once(w_refs, wbf_ref):
    @pl.when(pl.program_id(1) == 0)
    def _():
        for n, w_ref in enumerate(w_refs):
            wbf_ref[n] = w_ref[...].astype(wbf_ref.dtype)


def _conv_kernel(tiles_per_seq, rcs, width, x_ref, wb_ref, wc_ref, wh_ref, cw_ref, g_ref,
                 y_ref, inv_ref, wbf_ref, ext_ref, acc_ref):
    i = pl.program_id(1)
    tm, tn = y_ref.shape
    _cast_weights_once((wb_ref, wc_ref, wh_ref), wbf_ref)

    @pl.when(i % tiles_per_seq == 0)
    def _():
        ext_ref[0:HALO, :] = jnp.zeros((HALO, tn), jnp.float32)

    _zero_row_sums_at_start(acc_ref)
    for r0, rc in _chunks(rcs):
        rs = slice(r0, r0 + rc)
        xb = x_ref[rs, :]
        bg = _bdot(xb, wbf_ref[0])
        ch = _bdot(xb, wbf_ref[1]) * _bdot(xb, wbf_ref[2])
        base = HALO + r0
        ext_ref[base:base + rc, :] = ch
        conv = cw_ref[CONV_K - 1:CONV_K, :] * ch
        for k in range(CONV_K - 1):
            shift = CONV_K - 1 - k
            conv = conv + cw_ref[k:k + 1, :] * ext_ref[base - shift:base - shift + rc, :]
        ya = bg * conv
        y_ref[rs, :] = (ya * g_ref[...]).astype(y_ref.dtype)
        _accumulate_inv_rms(acc_ref, inv_ref, r0, _lane_group_sum(ya * ya), width)
    ext_ref[0:HALO, :] = ext_ref[tm:tm + HALO, :]


def _conv_mixer(xn, w_in, conv_w, g_a, seq, cw, tm=1024, tn=256,
                rcs=(256, 256, 256, 128, 128)):
    t, d = xn.shape
    nj = cw // tn
    assert sum(rcs) == tm
    kern = functools.partial(_conv_kernel, seq // tm, rcs, cw)
    wspec = lambda off: pl.BlockSpec((d, tn), lambda j, i, off=off: (0, j + off))
    return pl.pallas_call(
        kern,
        grid=(nj, t // tm),
        in_specs=[pl.BlockSpec((tm, d), lambda j, i: (i, 0)),
                  wspec(0), wspec(nj), wspec(2 * nj),
                  pl.BlockSpec((CONV_K, tn), lambda j, i: (0, j)),
                  pl.BlockSpec((1, tn), lambda j, i: (0, j))],
        out_specs=[pl.BlockSpec((tm, tn), lambda j, i: (i, j)), _inv_rms_spec(tm, nj - 1)],
        out_shape=[jax.ShapeDtypeStruct((t, cw), jnp.bfloat16),
                   jax.ShapeDtypeStruct((t, LANES), jnp.float32)],
        scratch_shapes=[pltpu.VMEM((3, d, tn), jnp.bfloat16),
                        pltpu.VMEM((HALO + tm, tn), jnp.float32),
                        pltpu.VMEM((t, LANES), jnp.float32)],
        compiler_params=_params(_ARB2),
        name="conv_mixer",
    )(xn, w_in, w_in, w_in, conv_w, g_a.reshape(1, cw))


def _gmlp_kernel(rcs, width, x_ref, wuc_ref, wvc_ref, sw_ref, sb_ref, g_ref, wo_ref,
                 y_ref, inv_ref, wob_ref, wbf_a, wbf_b, acc_ref):
    tm, tn = y_ref.shape
    kc = wuc_ref.shape[0]
    chunk_rows = pl.ds(pl.multiple_of(pl.program_id(1) * kc, kc), kc)
    wob_ref[...] = wo_ref[...].astype(wob_ref.dtype)
    _zero_row_sums_at_start(acc_ref)

    def step(cur_ref, nxt_ref):
        nxt_ref[0, chunk_rows, :] = wuc_ref[...].astype(nxt_ref.dtype)
        nxt_ref[1, chunk_rows, :] = wvc_ref[...].astype(nxt_ref.dtype)
        if cur_ref is None:
            return
        row = lax.broadcasted_iota(jnp.int32, (CHUNK, CHUNK), 0)
        col = lax.broadcasted_iota(jnp.int32, (CHUNK, CHUNK), 1)
        causal = col <= row
        heads = tn // HEAD_DIM
        ws = [jnp.where(causal, sw_ref[h], 0.0).astype(jnp.bfloat16) for h in range(heads)]
        chunks = _chunks(rcs)

        def uv_dots(r0, rc):
            xb = x_ref[r0:r0 + rc, :]
            return _bdot(xb, cur_ref[0]), _bdot(xb, cur_ref[1])

        uv = uv_dots(*chunks[0])
        for n, (r0, rc) in enumerate(chunks):
            u, v = uv
            if n + 1 < len(chunks):
                uv = uv_dots(*chunks[n + 1])
            gu = jax.nn.gelu(u)
            gv = jax.nn.gelu(v).astype(jnp.bfloat16)
            n_sub = rc // CHUNK
            gate = []
            for h in range(heads):
                hs = slice(h * HEAD_DIM, (h + 1) * HEAD_DIM)
                vcat = jnp.concatenate(
                    [gv[c * CHUNK:(c + 1) * CHUNK, hs] for c in range(n_sub)], axis=1)
                gate.append(_bdot(ws[h], vcat))
            s = jnp.concatenate(
                [jnp.concatenate([gate[h][:, c * CHUNK:(c + 1) * CHUNK] + sb_ref[h]
                                  for h in range(heads)], axis=1)
                 for c in range(n_sub)], axis=0)
            yb = gu * s
            y_ref[r0:r0 + rc, :] = (yb * g_ref[...]).astype(y_ref.dtype)
            _accumulate_inv_rms(acc_ref, inv_ref, r0, _lane_group_sum(yb * yb), width)

    _run_streamed_pass(step, wbf_a, wbf_b)


def _gmlp_mixer(xn, w_in, spatial_w, spatial_b, g_b, col0, gw, w_out, tm=1024, tn=512,
                rcs=(256, 256, 256, 128, 128)):
    t, d = xn.shape
    nj, ni = gw // tn, t // tm
    assert sum(rcs) == tm and all(rc % CHUNK == 0 for rc in rcs)
    hpt = tn // HEAD_DIM
    off_u = col0 // tn
    sb = jnp.broadcast_to(spatial_b[:, :, None], spatial_b.shape + (HEAD_DIM,))
    kern = functools.partial(_gmlp_kernel, rcs, gw)
    next_col = lambda p: jnp.minimum(p, nj - 1)
    wo_spec = _pass_slab_spec(w_out.shape[0], w_out.shape[1], nj, ni)
    return pl.pallas_call(
        kern,
        grid=(nj + 1, ni),
        in_specs=[pl.BlockSpec((tm, d), lambda p, i: (_pass_row_tile(p, i), 0)),
                  pl.BlockSpec((d // ni, tn), lambda p, i: (i, off_u + next_col(p))),
                  pl.BlockSpec((d // ni, tn), lambda p, i: (i, off_u + nj + next_col(p))),
                  pl.BlockSpec((hpt, CHUNK, CHUNK), lambda p, i: (_pass_col_tile(p), 0, 0)),
                  pl.BlockSpec((hpt, CHUNK, HEAD_DIM),
                               lambda p, i: (_pass_col_tile(p), 0, 0)),
                  pl.BlockSpec((1, tn), lambda p, i: (0, _pass_col_tile(p))),
                  wo_spec],
        out_specs=[pl.BlockSpec((tm, tn),
                                lambda p, i: (_pass_row_tile(p, i), _pass_col_tile(p))),
                   _inv_rms_spec(tm, nj),
                   wo_spec],
        out_shape=[jax.ShapeDtypeStruct((t, gw), jnp.bfloat16),
                   jax.ShapeDtypeStruct((t, LANES), jnp.float32),
                   jax.ShapeDtypeStruct(w_out.shape, jnp.bfloat16)],
        scratch_shapes=[pltpu.VMEM((2, d, tn), jnp.bfloat16),
                        pltpu.VMEM((2, d, tn), jnp.bfloat16),
                        pltpu.VMEM((t, LANES), jnp.float32)],
        compiler_params=_params(_ARB2),
        name="gmlp_mixer",
    )(xn, w_in, w_in, spatial_w, sb, g_b.reshape(1, gw), w_out)


def _out_proj_kernel(ccs, width, ya_ref, yb_ref, wa_ref, wb_ref, ia_ref, ib_ref, x_ref, g_ref,
                     h_ref, xg_ref, inv_ref, acc_ref):
    tm, tn = h_ref.shape
    _zero_row_sums_at_start(acc_ref)
    ya, yb = ya_ref[...], yb_ref[...]
    sq = None
    for c0, cc in _chunks(ccs):
        cs = slice(c0, c0 + cc)
        pa = _bdot(ya, wa_ref[:, cs]) * _rep_lanes(ia_ref[...], cc)
        pb = _bdot(yb, wb_ref[:, cs]) * _rep_lanes(ib_ref[...], cc)
        h = x_ref[:, cs] + pa + pb
        h_ref[:, cs] = h
        xg_ref[:, cs] = (h * g_ref[:, cs]).astype(xg_ref.dtype)
        part = _lane_group_sum(h * h)
        sq = part if sq is None else sq + part
    _accumulate_inv_rms(acc_ref, inv_ref, 0, sq, width)


def _out_proj(ya, yb, w_out, inv_a, inv_b, x, g_mlp, tm=256, tn=2048,
              ccs=(512, 512, 512, 256, 256)):
    t, cw = ya.shape
    gw = yb.shape[1]
    d = w_out.shape[1]
    assert cw == gw and sum(ccs) == tn
    kern = functools.partial(_out_proj_kernel, ccs, d)
    row_spec = pl.BlockSpec((tm, LANES), lambda j, i: (i, 0))
    tile_spec = pl.BlockSpec((tm, tn), lambda j, i: (i, j))
    return pl.pallas_call(
        kern,
        grid=(d // tn, t // tm),
        in_specs=[pl.BlockSpec((tm, cw), lambda j, i: (i, 0)),
                  pl.BlockSpec((tm, gw), lambda j, i: (i, 0)),
                  pl.BlockSpec((cw, tn), lambda j, i: (0, j)),
                  pl.BlockSpec((gw, tn), lambda j, i: (1, j)),
                  row_spec, row_spec, tile_spec,
                  pl.BlockSpec((1, tn), lambda j, i: (0, j))],
        out_specs=[tile_spec, tile_spec, _inv_rms_spec(tm, d // tn - 1)],
        out_shape=[jax.ShapeDtypeStruct((t, d), jnp.float32),
                   jax.ShapeDtypeStruct((t, d), jnp.bfloat16),
                   jax.ShapeDtypeStruct((t, LANES), jnp.float32)],
        scratch_shapes=[pltpu.VMEM((t, LANES), jnp.float32)],
        compiler_params=_params(_ARB2),
        name="out_proj",
    )(ya, yb, w_out, w_out, inv_a, inv_b, x, g_mlp.reshape(1, d))


def _mlp_up_kernel(rc, x_ref, wc_ref, wd_ref, r_ref, wdb_ref, wbf_a, wbf_b):
    kc = wc_ref.shape[0]
    chunk_rows = pl.ds(pl.multiple_of(pl.program_id(1) * kc, kc), kc)
    wdb_ref[...] = wd_ref[...].astype(wdb_ref.dtype)

    def step(cur_ref, nxt_ref):
        nxt_ref[chunk_rows, :] = wc_ref[...].astype(nxt_ref.dtype)
        if cur_ref is None:
            return
        for r in range(r_ref.shape[0] // rc):
            rs = slice(r * rc, (r + 1) * rc)
            z = jnp.maximum(_bdot(x_ref[rs, :], cur_ref[...]), 0.0)
            r_ref[rs, :] = (z * z).astype(r_ref.dtype)

    _run_streamed_pass(step, wbf_a, wbf_b)


def _mlp_up(xg, w_up, w_down, tm=1024, tn=1024, rc=1024):
    t, d = xg.shape
    f = w_up.shape[1]
    nj, ni = f // tn, t // tm
    kern = functools.partial(_mlp_up_kernel, rc)
    slab_spec = _pass_slab_spec(f, d, nj, ni)
    return pl.pallas_call(
        kern,
        grid=(nj + 1, ni),
        in_specs=[pl.BlockSpec((tm, d), lambda p, i: (_pass_row_tile(p, i), 0)),
                  pl.BlockSpec((d // ni, tn), lambda p, i: (i, jnp.minimum(p, nj - 1))),
                  slab_spec],
        out_specs=[pl.BlockSpec((tm, tn),
                                lambda p, i: (_pass_row_tile(p, i), _pass_col_tile(p))),
                   slab_spec],
        out_shape=[jax.ShapeDtypeStruct((t, f), jnp.bfloat16),
                   jax.ShapeDtypeStruct(w_down.shape, jnp.bfloat16)],
        scratch_shapes=[pltpu.VMEM((d, tn), jnp.bfloat16),
                        pltpu.VMEM((d, tn), jnp.bfloat16)],
        compiler_params=_params(_ARB2),
        name="mlp_up",
    )(xg, w_up, w_down)


def _mlp_down_kernel(r_ref, w_ref, h_ref, inv_ref, o_ref):
    k = pl.program_id(2)
    last = pl.num_programs(2) - 1

    @pl.when(k == 0)
    def _():
        o_ref[...] = _bdot(r_ref[...], w_ref[...])

    @pl.when((k > 0) & (k < last))
    def _():
        o_ref[...] += _bdot(r_ref[...], w_ref[...])

    @pl.when(k == last)
    def _():
        inv = _rep_lanes(inv_ref[...], o_ref.shape[1])
        o_ref[...] = h_ref[...] + inv * inv * (o_ref[...] + _bdot(r_ref[...], w_ref[...]))


def _mlp_down(r, w_down, h, inv, tm=1024, tn=1024, tk=4096):
    t, f = r.shape
    d = w_down.shape[1]
    assert f // tk >= 2
    return pl.pallas_call(
        _mlp_down_kernel,
        grid=(d // tn, t // tm, f // tk),
        in_specs=[pl.BlockSpec((tm, tk), lambda j, i, k: (i, k)),
                  pl.BlockSpec((tk, tn), lambda j, i, k: (k, j)),
                  pl.BlockSpec((tm, tn), lambda j, i, k: (i, j)),
                  pl.BlockSpec((tm, LANES), lambda j, i, k: (i, 0))],
        out_specs=pl.BlockSpec((tm, tn), lambda j, i, k: (i, j)),
        out_shape=jax.ShapeDtypeStruct((t, d), jnp.float32),
        compiler_params=_params(("arbitrary", "arbitrary", "arbitrary")),
        name="mlp_down",
    )(r, w_down, h, inv)


def kernel(x, mix_norm_g, w_in, conv_w, spatial_w, spatial_b, conv_out_norm_g,
           gmlp_out_norm_g, w_out, mlp_norm_g, w_up, w_down, final_norm_g):
    bsz, seq, d = x.shape
    depth = w_in.shape[0]
    cw = conv_w.shape[2]
    gw = spatial_w.shape[1] * HEAD_DIM
    assert seq % CHUNK == 0 and w_in.shape[2] == 3 * cw + 2 * gw
    h = x.reshape(bsz * seq, d)
    for l in range(depth):
        xn = _rmsnorm(h, mix_norm_g[l], jnp.bfloat16)
        ya, inv_a = _conv_mixer(xn, w_in[l], conv_w[l], conv_out_norm_g[l], seq, cw)
        yb, inv_b, w_out_bf16 = _gmlp_mixer(xn, w_in[l], spatial_w[l], spatial_b[l],
                                            gmlp_out_norm_g[l], 3 * cw, gw, w_out[l])
        h, xg, inv_h = _out_proj(ya, yb, w_out_bf16, inv_a, inv_b, h, mlp_norm_g[l])
        r, w_down_bf16 = _mlp_up(xg, w_up[l], w_down[l])
        h = _mlp_down(r, w_down_bf16, h, inv_h)
    out = _rmsnorm(h, final_norm_g, x.dtype)
    return out.reshape(bsz, seq, d)
```

```python
import functools
import math

import jax
import jax.numpy as jnp
from jax import lax
from jax.experimental import pallas as pl
from jax.experimental.pallas import tpu as pltpu

EPS = 1e-5
HEAD_DIM = 128
CHUNK = 128
CONV_K = 3
LANES = 128
HALO = 8

_VMEM_LIMIT = 60000 * 1024

_ARB2 = ("arbitrary", "arbitrary")


def _params(sem):
    return pltpu.CompilerParams(dimension_semantics=sem, vmem_limit_bytes=_VMEM_LIMIT)


def _bdot(a, b):
    return jnp.dot(a.astype(jnp.bfloat16), b.astype(jnp.bfloat16),
                   preferred_element_type=jnp.float32)


def _lane_group_sum(sq):
    acc = sq[:, 0:LANES]
    for c in range(1, sq.shape[1] // LANES):
        acc = acc + sq[:, c * LANES:(c + 1) * LANES]
    return acc


def _rep_lanes(v, width):
    return jnp.concatenate([v] * (width // LANES), axis=1)


def _zero_row_sums_at_start(acc_ref):
    @pl.when((pl.program_id(0) == 0) & (pl.program_id(1) == 0))
    def _():
        acc_ref[...] = jnp.zeros(acc_ref.shape, acc_ref.dtype)


def _inv_rms_spec(tm, last):
    return pl.BlockSpec((tm, LANES), lambda j, i: (jnp.where(j == last, i, 0), 0))


_WHOLE_VMEM = pl.BlockSpec(memory_space=pltpu.VMEM)


def _chunks(sizes):
    out, start = [], 0
    for size in sizes:
        out.append((start, size))
        start += size
    return out


def _pass_row_tile(p, i):
    return jnp.where(p == 0, 0, i)


def _pass_col_tile(p):
    return jnp.maximum(p - 1, 0)


def _pass_slab_spec(rows, cols, nj, ni):
    return pl.BlockSpec((rows // (nj * ni), cols),
                        lambda p, i: (_pass_col_tile(p) * ni + _pass_row_tile(p, i), 0))


def _run_streamed_pass(step, wbf_a, wbf_b):
    p = pl.program_id(0)
    pl.when(p == 0)(lambda: step(None, wbf_a))
    pl.when((p > 0) & (p % 2 == 1))(lambda: step(wbf_a, wbf_b))
    pl.when((p > 0) & (p % 2 == 0))(lambda: step(wbf_b, wbf_a))


def _accumulate_inv_rms(acc_ref, inv_ref, row0, sq, width):
    rc = sq.shape[0]
    tm = inv_ref.shape[0]
    rows = pl.ds(pl.multiple_of(pl.program_id(1) * tm + row0, math.gcd(tm, row0)), rc)
    ssq = acc_ref[rows, :] + sq
    acc_ref[rows, :] = ssq
    total = jnp.sum(ssq, axis=-1, keepdims=True)
    inv_ref[row0:row0 + rc, :] = jnp.broadcast_to(lax.rsqrt(total / width + EPS),
                                                  (rc, inv_ref.shape[1]))


def _rmsnorm_kernel(x_ref, g_ref, o_ref):
    x = x_ref[...]
    ms = jnp.mean(x * x, axis=-1, keepdims=True)
    o_ref[...] = (x * lax.rsqrt(ms + EPS) * g_ref[...]).astype(o_ref.dtype)


def _rmsnorm(x, g, out_dtype, tr=512):
    t, d = x.shape
    return pl.pallas_call(
        _rmsnorm_kernel,
        grid=(t // tr,),
        in_specs=[pl.BlockSpec((tr, d), lambda i: (i, 0)),
                  pl.BlockSpec((1, d), lambda i: (0, 0))],
        out_specs=pl.BlockSpec((tr, d), lambda i: (i, 0)),
        out_shape=jax.ShapeDtypeStruct((t, d), out_dtype),
        compiler_params=_params(("arbitrary",)),
        name="rmsnorm_cast",
    )(x, g.reshape(1, d))


def _cast_weights_once(w_refs, wbf_ref):
    @pl.when(pl.program_id(1) == 0)
    def _():
        for n, w_ref in enumerate(w_refs):
            wbf_ref[n] = w_ref[...].astype(wbf_ref.dtype)


def _conv_kernel(tiles_per_seq, rcs, width, x_ref, wb_ref, wc_ref, wh_ref, cw_ref, g_ref,
                 y_ref, inv_ref, wbf_ref, ext_ref, acc_ref):
    j, i = pl.program_id(0), pl.program_id(1)
    tm, tn = y_ref.shape
    taps = [cw_ref[k, pl.ds(j, 1), :] for k in range(CONV_K)]
    g = g_ref[pl.ds(j, 1), :]
    _cast_weights_once((wb_ref, wc_ref, wh_ref), wbf_ref)

    @pl.when(i % tiles_per_seq == 0)
    def _():
        ext_ref[0:HALO, :] = jnp.zeros((HALO, tn), jnp.float32)

    _zero_row_sums_at_start(acc_ref)
    for r0, rc in _chunks(rcs):
        rs = slice(r0, r0 + rc)
        xb = x_ref[rs, :]
        bg = _bdot(xb, wbf_ref[0])
        ch = _bdot(xb, wbf_ref[1]) * _bdot(xb, wbf_ref[2])
        base = HALO + r0
        ext_ref[base:base + rc, :] = ch
        conv = taps[CONV_K - 1] * ch
        for k in range(CONV_K - 1):
            shift = CONV_K - 1 - k
            conv = conv + taps[k] * ext_ref[base - shift:base - shift + rc, :]
        ya = bg * conv
        y_ref[rs, :] = (ya * g).astype(y_ref.dtype)
        _accumulate_inv_rms(acc_ref, inv_ref, r0, _lane_group_sum(ya * ya), width)
    ext_ref[0:HALO, :] = ext_ref[tm:tm + HALO, :]


def _conv_mixer(xn, w_in, conv_w, g_a, seq, cw, tm=1024, tn=256,
                rcs=(256, 256, 256, 128, 128)):
    t, d = xn.shape
    nj = cw // tn
    assert sum(rcs) == tm
    kern = functools.partial(_conv_kernel, seq // tm, rcs, cw)
    wspec = lambda off: pl.BlockSpec((d, tn), lambda j, i, off=off: (0, j + off))
    return pl.pallas_call(
        kern,
        grid=(nj, t // tm),
        in_specs=[pl.BlockSpec((tm, d), lambda j, i: (i, 0)),
                  wspec(0), wspec(nj), wspec(2 * nj), _WHOLE_VMEM, _WHOLE_VMEM],
        out_specs=[pl.BlockSpec((tm, tn), lambda j, i: (i, j)), _inv_rms_spec(tm, nj - 1)],
        out_shape=[jax.ShapeDtypeStruct((t, cw), jnp.bfloat16),
                   jax.ShapeDtypeStruct((t, LANES), jnp.float32)],
        scratch_shapes=[pltpu.VMEM((3, d, tn), jnp.bfloat16),
                        pltpu.VMEM((HALO + tm, tn), jnp.float32),
                        pltpu.VMEM((t, LANES), jnp.float32)],
        compiler_params=_params(_ARB2),
        name="conv_mixer",
    )(xn, w_in, w_in, w_in, conv_w.reshape(CONV_K, nj, tn), g_a.reshape(nj, tn))


def _gmlp_kernel(rcs, width, x_ref, wuc_ref, wvc_ref, sw_ref, sb_ref, g_ref, wo_ref,
                 y_ref, inv_ref, wob_ref, wbf_a, wbf_b, acc_ref):
    tm, tn = y_ref.shape
    kc = wuc_ref.shape[0]
    chunk_rows = pl.ds(pl.multiple_of(pl.program_id(1) * kc, kc), kc)
    wob_ref[...] = wo_ref[...].astype(wob_ref.dtype)
    _zero_row_sums_at_start(acc_ref)

    def step(cur_ref, nxt_ref):
        nxt_ref[0, chunk_rows, :] = wuc_ref[...].astype(nxt_ref.dtype)
        nxt_ref[1, chunk_rows, :] = wvc_ref[...].astype(nxt_ref.dtype)
        if cur_ref is None:
            return
        col_tile = pl.program_id(0) - 1
        row = lax.broadcasted_iota(jnp.int32, (CHUNK, CHUNK), 0)
        col = lax.broadcasted_iota(jnp.int32, (CHUNK, CHUNK), 1)
        causal = col <= row
        heads = tn // HEAD_DIM
        head0 = col_tile * heads
        ws = [jnp.where(causal, sw_ref[head0 + h], 0.0).astype(jnp.bfloat16)
              for h in range(heads)]
        bias = [sb_ref[head0 + h] for h in range(heads)]
        g = g_ref[pl.ds(col_tile, 1), :]
        chunks = _chunks(rcs)

        def uv_dots(r0, rc):
            xb = x_ref[r0:r0 + rc, :]
            return _bdot(xb, cur_ref[0]), _bdot(xb, cur_ref[1])

        uv = uv_dots(*chunks[0])
        for n, (r0, rc) in enumerate(chunks):
            u, v = uv
            if n + 1 < len(chunks):
                uv = uv_dots(*chunks[n + 1])
            gu = jax.nn.gelu(u)
            gv = jax.nn.gelu(v).astype(jnp.bfloat16)
            n_sub = rc // CHUNK
            gate = []
            for h in range(heads):
                hs = slice(h * HEAD_DIM, (h + 1) * HEAD_DIM)
                vcat = jnp.concatenate(
                    [gv[c * CHUNK:(c + 1) * CHUNK, hs] for c in range(n_sub)], axis=1)
                gate.append(_bdot(ws[h], vcat))
            s = jnp.concatenate(
                [jnp.concatenate([gate[h][:, c * CHUNK:(c + 1) * CHUNK] + bias[h]
                                  for h in range(heads)], axis=1)
                 for c in range(n_sub)], axis=0)
            yb = gu * s
            y_ref[r0:r0 + rc, :] = (yb * g).astype(y_ref.dtype)
            _accumulate_inv_rms(acc_ref, inv_ref, r0, _lane_group_sum(yb * yb), width)

    _run_streamed_pass(step, wbf_a, wbf_b)


def _gmlp_mixer(xn, w_in, spatial_w, spatial_b, g_b, col0, gw, w_out, tm=1024, tn=512,
                rcs=(256, 256, 256, 128, 128)):
    t, d = xn.shape
    nj, ni = gw // tn, t // tm
    assert sum(rcs) == tm and all(rc % CHUNK == 0 for rc in rcs)
    off_u = col0 // tn
    sb = jnp.broadcast_to(spatial_b[:, :, None], spatial_b.shape + (HEAD_DIM,))
    kern = functools.partial(_gmlp_kernel, rcs, gw)
    next_col = lambda p: jnp.minimum(p, nj - 1)
    wo_spec = _pass_slab_spec(w_out.shape[0], w_out.shape[1], nj, ni)
    return pl.pallas_call(
        kern,
        grid=(nj + 1, ni),
        in_specs=[pl.BlockSpec((tm, d), lambda p, i: (_pass_row_tile(p, i), 0)),
                  pl.BlockSpec((d // ni, tn), lambda p, i: (i, off_u + next_col(p))),
                  pl.BlockSpec((d // ni, tn), lambda p, i: (i, off_u + nj + next_col(p))),
                  _WHOLE_VMEM, _WHOLE_VMEM, _WHOLE_VMEM,
                  wo_spec],
        out_specs=[pl.BlockSpec((tm, tn),
                                lambda p, i: (_pass_row_tile(p, i), _pass_col_tile(p))),
                   _inv_rms_spec(tm, nj),
                   wo_spec],
        out_shape=[jax.ShapeDtypeStruct((t, gw), jnp.bfloat16),
                   jax.ShapeDtypeStruct((t, LANES), jnp.float32),
                   jax.ShapeDtypeStruct(w_out.shape, jnp.bfloat16)],
        scratch_shapes=[pltpu.VMEM((2, d, tn), jnp.bfloat16),
                        pltpu.VMEM((2, d, tn), jnp.bfloat16),
                        pltpu.VMEM((t, LANES), jnp.float32)],
        compiler_params=_params(_ARB2),
        name="gmlp_mixer",
    )(xn, w_in, w_in, spatial_w, sb, g_b.reshape(nj, tn), w_out)


def _out_proj_kernel(ccs, width, ya_ref, yb_ref, wa_ref, wb_ref, ia_ref, ib_ref, x_ref, g_ref,
                     h_ref, xg_ref, inv_ref, acc_ref):
    tm, tn = h_ref.shape
    _zero_row_sums_at_start(acc_ref)
    ya, yb = ya_ref[...], yb_ref[...]
    g = g_ref[pl.ds(pl.program_id(0), 1), :]
    sq = None
    for c0, cc in _chunks(ccs):
        cs = slice(c0, c0 + cc)
        pa = _bdot(ya, wa_ref[:, cs]) * _rep_lanes(ia_ref[...], cc)
        pb = _bdot(yb, wb_ref[:, cs]) * _rep_lanes(ib_ref[...], cc)
        h = x_ref[:, cs] + pa + pb
        h_ref[:, cs] = h
        xg_ref[:, cs] = (h * g[:, cs]).astype(xg_ref.dtype)
        part = _lane_group_sum(h * h)
        sq = part if sq is None else sq + part
    _accumulate_inv_rms(acc_ref, inv_ref, 0, sq, width)


def _out_proj(ya, yb, w_out, inv_a, inv_b, x, g_mlp, tm=512, tn=2048,
              ccs=(512, 512, 512, 256, 256)):
    t, cw = ya.shape
    gw = yb.shape[1]
    d = w_out.shape[1]
    assert cw == gw and sum(ccs) == tn
    kern = functools.partial(_out_proj_kernel, ccs, d)
    row_spec = pl.BlockSpec((tm, LANES), lambda j, i: (i, 0))
    tile_spec = pl.BlockSpec((tm, tn), lambda j, i: (i, j))
    return pl.pallas_call(
        kern,
        grid=(d // tn, t // tm),
        in_specs=[pl.BlockSpec((tm, cw), lambda j, i: (i, 0)),
                  pl.BlockSpec((tm, gw), lambda j, i: (i, 0)),
                  pl.BlockSpec((cw, tn), lambda j, i: (0, j), pipeline_mode=pl.Buffered(1)),
                  pl.BlockSpec((gw, tn), lambda j, i: (1, j), pipeline_mode=pl.Buffered(1)),
                  row_spec, row_spec, tile_spec, _WHOLE_VMEM],
        out_specs=[tile_spec, tile_spec, _inv_rms_spec(tm, d // tn - 1)],
        out_shape=[jax.ShapeDtypeStruct((t, d), jnp.float32),
                   jax.ShapeDtypeStruct((t, d), jnp.bfloat16),
                   jax.ShapeDtypeStruct((t, LANES), jnp.float32)],
        scratch_shapes=[pltpu.VMEM((t, LANES), jnp.float32)],
        compiler_params=_params(_ARB2),
        name="out_proj",
    )(ya, yb, w_out, w_out, inv_a, inv_b, x, g_mlp.reshape(d // tn, tn))


def _mlp_up_kernel(rc, x_ref, wc_ref, wd_ref, r_ref, wdb_ref, wbf_a, wbf_b):
    kc = wc_ref.shape[0]
    chunk_rows = pl.ds(pl.multiple_of(pl.program_id(1) * kc, kc), kc)
    wdb_ref[...] = wd_ref[...].astype(wdb_ref.dtype)

    def step(cur_ref, nxt_ref):
        nxt_ref[chunk_rows, :] = wc_ref[...].astype(nxt_ref.dtype)
        if cur_ref is None:
            return
        for r in range(r_ref.shape[0] // rc):
            rs = slice(r * rc, (r + 1) * rc)
            z = jnp.maximum(_bdot(x_ref[rs, :], cur_ref[...]), 0.0)
            r_ref[rs, :] = (z * z).astype(r_ref.dtype)

    _run_streamed_pass(step, wbf_a, wbf_b)


def _mlp_up(xg, w_up, w_down, tm=1024, tn=1024, rc=1024):
    t, d = xg.shape
    f = w_up.shape[1]
    nj, ni = f // tn, t // tm
    kern = functools.partial(_mlp_up_kernel, rc)
    slab_spec = _pass_slab_spec(f, d, nj, ni)
    return pl.pallas_call(
        kern,
        grid=(nj + 1, ni),
        in_specs=[pl.BlockSpec((tm, d), lambda p, i: (_pass_row_tile(p, i), 0)),
                  pl.BlockSpec((d // ni, tn), lambda p, i: (i, jnp.minimum(p, nj - 1))),
                  slab_spec],
        out_specs=[pl.BlockSpec((tm, tn),
                                lambda p, i: (_pass_row_tile(p, i), _pass_col_tile(p))),
                   slab_spec],
        out_shape=[jax.ShapeDtypeStruct((t, f), jnp.bfloat16),
                   jax.ShapeDtypeStruct(w_down.shape, jnp.bfloat16)],
        scratch_shapes=[pltpu.VMEM((d, tn), jnp.bfloat16),
                        pltpu.VMEM((d, tn), jnp.bfloat16)],
        compiler_params=_params(_ARB2),
        name="mlp_up",
    )(xg, w_up, w_down)


def _mlp_down_kernel(r_ref, w_ref, h_ref, inv_ref, o_ref):
    k = pl.program_id(2)
    last = pl.num_programs(2) - 1

    @pl.when(k == 0)
    def _():
        o_ref[...] = _bdot(r_ref[...], w_ref[...])

    @pl.when((k > 0) & (k < last))
    def _():
        o_ref[...] += _bdot(r_ref[...], w_ref[...])

    @pl.when(k == last)
    def _():
        inv = _rep_lanes(inv_ref[...], o_ref.shape[1])
        o_ref[...] = h_ref[...] + inv * inv * (o_ref[...] + _bdot(r_ref[...], w_ref[...]))


def _mlp_down(r, w_down, h, inv, tm=1024, tn=1024, tk=4096):
    t, f = r.shape
    d = w_down.shape[1]
    assert f // tk >= 2
    return pl.pallas_call(
        _mlp_down_kernel,
        grid=(d // tn, t // tm, f // tk),
        in_specs=[pl.BlockSpec((tm, tk), lambda j, i, k: (i, k)),
                  pl.BlockSpec((tk, tn), lambda j, i, k: (k, j)),
                  pl.BlockSpec((tm, tn), lambda j, i, k: (i, j)),
                  pl.BlockSpec((tm, LANES), lambda j, i, k: (i, 0))],
        out_specs=pl.BlockSpec((tm, tn), lambda j, i, k: (i, j)),
        out_shape=jax.ShapeDtypeStruct((t, d), jnp.float32),
        compiler_params=_params(("arbitrary", "arbitrary", "arbitrary")),
        name="mlp_down",
    )(r, w_down, h, inv)


def kernel(x, mix_norm_g, w_in, conv_w, spatial_w, spatial_b, conv_out_norm_g,
           gmlp_out_norm_g, w_out, mlp_norm_g, w_up, w_down, final_norm_g):
    bsz, seq, d = x.shape
    depth = w_in.shape[0]
    cw = conv_w.shape[2]
    gw = spatial_w.shape[1] * HEAD_DIM
    assert seq % CHUNK == 0 and w_in.shape[2] == 3 * cw + 2 * gw
    h = x.reshape(bsz * seq, d)
    for l in range(depth):
        xn = _rmsnorm(h, mix_norm_g[l], jnp.bfloat16)
        ya, inv_a = _conv_mixer(xn, w_in[l], conv_w[l], conv_out_norm_g[l], seq, cw)
        yb, inv_b, w_out_bf16 = _gmlp_mixer(xn, w_in[l], spatial_w[l], spatial_b[l],
                                            gmlp_out_norm_g[l], 3 * cw, gw, w_out[l])
        h, xg, inv_h = _out_proj(ya, yb, w_out_bf16, inv_a, inv_b, h, mlp_norm_g[l])
        r, w_down_bf16 = _mlp_up(xg, w_up[l], w_down[l])
        h = _mlp_down(r, w_down_bf16, h, inv_h)
    out = _rmsnorm(h, final_norm_g, x.dtype)
    return out.reshape(bsz, seq, d)
```

```python
import functools

import jax
import jax.numpy as jnp
from jax import lax
from jax.experimental import pallas as pl
from jax.experimental.pallas import tpu as pltpu

EPS = 1e-5
HEAD_DIM = 128
CHUNK = 128
CONV_K = 3
LANES = 128
HALO = 8

_VMEM_LIMIT = 60000 * 1024

_ARB2 = ("arbitrary", "arbitrary")


def _params(sem):
    return pltpu.CompilerParams(dimension_semantics=sem, vmem_limit_bytes=_VMEM_LIMIT)


def _bdot(a, b):
    return jnp.dot(a.astype(jnp.bfloat16), b.astype(jnp.bfloat16),
                   preferred_element_type=jnp.float32)


def _lane_group_sum(sq):
    acc = sq[:, 0:LANES]
    for c in range(1, sq.shape[1] // LANES):
        acc = acc + sq[:, c * LANES:(c + 1) * LANES]
    return acc


def _ssq_spec(tm, row_tile, col_tile):
    return pl.BlockSpec((1, tm, LANES), lambda a, i: (col_tile(a), row_tile(a, i), 0))


def _inv_rms(ssq_ref, width):
    total = jnp.sum(jnp.sum(ssq_ref[...], axis=0), axis=-1, keepdims=True)
    return lax.rsqrt(total / width + EPS)


_WHOLE_VMEM = pl.BlockSpec(memory_space=pltpu.VMEM)


def _chunks(sizes):
    out, start = [], 0
    for size in sizes:
        out.append((start, size))
        start += size
    return out


def _pass_row_tile(p, i):
    return jnp.where(p == 0, 0, i)


def _pass_col_tile(p):
    return jnp.maximum(p - 1, 0)


def _pass_slab_spec(rows, cols, nj, ni):
    return pl.BlockSpec((rows // (nj * ni), cols),
                        lambda p, i: (_pass_col_tile(p) * ni + _pass_row_tile(p, i), 0))


def _run_streamed_pass(step, wbf_a, wbf_b):
    p = pl.program_id(0)
    pl.when(p == 0)(lambda: step(None, wbf_a))
    pl.when((p > 0) & (p % 2 == 1))(lambda: step(wbf_a, wbf_b))
    pl.when((p > 0) & (p % 2 == 0))(lambda: step(wbf_b, wbf_a))


def _rmsnorm_kernel(x_ref, g_ref, o_ref):
    x = x_ref[...]
    ms = jnp.mean(x * x, axis=-1, keepdims=True)
    o_ref[...] = (x * lax.rsqrt(ms + EPS) * g_ref[...]).astype(o_ref.dtype)


def _rmsnorm(x, g, out_dtype, tr=512):
    t, d = x.shape
    return pl.pallas_call(
        _rmsnorm_kernel,
        grid=(t // tr,),
        in_specs=[pl.BlockSpec((tr, d), lambda i: (i, 0)),
                  pl.BlockSpec((1, d), lambda i: (0, 0))],
        out_specs=pl.BlockSpec((tr, d), lambda i: (i, 0)),
        out_shape=jax.ShapeDtypeStruct((t, d), out_dtype),
        compiler_params=_params(("arbitrary",)),
        name="rmsnorm_cast",
    )(x, g.reshape(1, d))


def _conv_kernel(tiles_per_seq, rcs, cc, x_ref, wbc_ref, wcc_ref, whc_ref, cw_ref, g_ref,
                 y_ref, ssq_ref, wbf_a, wbf_b, ext_ref):
    tm, tn = y_ref.shape
    kc = wbc_ref.shape[0]
    chunk_rows = pl.ds(pl.multiple_of(pl.program_id(1) * kc, kc), kc)

    def step(cur_ref, nxt_ref):
        for n, wc_ref in enumerate((wbc_ref, wcc_ref, whc_ref)):
            nxt_ref[n, chunk_rows, :] = wc_ref[...].astype(nxt_ref.dtype)
        if cur_ref is None:
            ext_ref[0:HALO, :] = jnp.zeros((HALO, tn), jnp.float32)
            return
        col_tile = pl.program_id(0) - 1
        taps = [cw_ref[k, pl.ds(col_tile, 1), :] for k in range(CONV_K)]
        g = g_ref[pl.ds(col_tile, 1), :]
        for r0, rc in _chunks(rcs):
            rs = slice(r0, r0 + rc)
            xb = x_ref[rs, :]
            base = HALO + r0
            sq = None
            for c0 in range(0, tn, cc):
                cs = slice(c0, c0 + cc)
                bg = _bdot(xb, cur_ref[0, :, cs])
                ch = _bdot(xb, cur_ref[1, :, cs]) * _bdot(xb, cur_ref[2, :, cs])
                ext_ref[base:base + rc, cs] = ch
                conv = taps[CONV_K - 1][:, cs] * ch
                for k in range(CONV_K - 1):
                    shift = CONV_K - 1 - k
                    conv = conv + (taps[k][:, cs]
                                   * ext_ref[base - shift:base - shift + rc, cs])
                ya = bg * conv
                y_ref[rs, cs] = (ya * g[:, cs]).astype(y_ref.dtype)
                part = _lane_group_sum(ya * ya)
                sq = part if sq is None else sq + part
            ssq_ref[0, rs, :] = sq
        next_starts_sequence = (pl.program_id(1) + 1) % tiles_per_seq == 0
        ext_ref[0:HALO, :] = jnp.where(next_starts_sequence, 0.0, ext_ref[tm:tm + HALO, :])

    _run_streamed_pass(step, wbf_a, wbf_b)


def _conv_mixer(xn, w_in, conv_w, g_a, seq, cw, tm=1024, tn=512,
                rcs=(256, 256, 256, 128, 128), cc=256):
    t, d = xn.shape
    nj, ni = cw // tn, t // tm
    assert sum(rcs) == tm and tn % cc == 0
    kern = functools.partial(_conv_kernel, seq // tm, rcs, cc)
    next_col = lambda p: jnp.minimum(p, nj - 1)
    wspec = lambda off: pl.BlockSpec((d // ni, tn),
                                     lambda p, i, off=off: (i, off + next_col(p)))
    return pl.pallas_call(
        kern,
        grid=(nj + 1, ni),
        in_specs=[pl.BlockSpec((tm, d), lambda p, i: (_pass_row_tile(p, i), 0)),
                  wspec(0), wspec(nj), wspec(2 * nj), _WHOLE_VMEM, _WHOLE_VMEM],
        out_specs=[pl.BlockSpec((tm, tn),
                                lambda p, i: (_pass_row_tile(p, i), _pass_col_tile(p))),
                   _ssq_spec(tm, _pass_row_tile, _pass_col_tile)],
        out_shape=[jax.ShapeDtypeStruct((t, cw), jnp.bfloat16),
                   jax.ShapeDtypeStruct((nj, t, LANES), jnp.float32)],
        scratch_shapes=[pltpu.VMEM((3, d, tn), jnp.bfloat16),
                        pltpu.VMEM((3, d, tn), jnp.bfloat16),
                        pltpu.VMEM((HALO + tm, tn), jnp.float32)],
        compiler_params=_params(_ARB2),
        name="conv_mixer",
    )(xn, w_in, w_in, w_in, conv_w.reshape(CONV_K, nj, tn), g_a.reshape(nj, tn))


def _gmlp_kernel(rcs, cc, x_ref, wuc_ref, wvc_ref, sw_ref, sb_ref, g_ref, wo_ref,
                 y_ref, ssq_ref, wob_ref, wbf_a, wbf_b):
    tm, tn = y_ref.shape
    kc = wuc_ref.shape[0]
    chunk_rows = pl.ds(pl.multiple_of(pl.program_id(1) * kc, kc), kc)
    wob_ref[...] = wo_ref[...].astype(wob_ref.dtype)

    def step(cur_ref, nxt_ref):
        nxt_ref[0, chunk_rows, :] = wuc_ref[...].astype(nxt_ref.dtype)
        nxt_ref[1, chunk_rows, :] = wvc_ref[...].astype(nxt_ref.dtype)
        if cur_ref is None:
            return
        col_tile = pl.program_id(0) - 1
        row = lax.broadcasted_iota(jnp.int32, (CHUNK, CHUNK), 0)
        col = lax.broadcasted_iota(jnp.int32, (CHUNK, CHUNK), 1)
        causal = col <= row
        heads = tn // HEAD_DIM
        head0 = col_tile * heads
        ws = [jnp.where(causal, sw_ref[head0 + h], 0.0).astype(jnp.bfloat16)
              for h in range(heads)]
        bias = [sb_ref[head0 + h] for h in range(heads)]
        g = g_ref[pl.ds(col_tile, 1), :]
        units = [(r0, rc, c0) for r0, rc in _chunks(rcs) for c0 in range(0, tn, cc)]
        sub_heads = cc // HEAD_DIM

        def uv_dots(r0, rc, c0):
            xb = x_ref[r0:r0 + rc, :]
            return (_bdot(xb, cur_ref[0, :, c0:c0 + cc]),
                    _bdot(xb, cur_ref[1, :, c0:c0 + cc]))

        uv = uv_dots(*units[0])
        sq = None
        for n, (r0, rc, c0) in enumerate(units):
            u, v = uv
            if n + 1 < len(units):
                uv = uv_dots(*units[n + 1])
            gu = jax.nn.gelu(u)
            gv = jax.nn.gelu(v).astype(jnp.bfloat16)
            n_sub = rc // CHUNK
            h0 = c0 // HEAD_DIM
            gate = []
            for h in range(sub_heads):
                hs = slice(h * HEAD_DIM, (h + 1) * HEAD_DIM)
                vcat = jnp.concatenate(
                    [gv[c * CHUNK:(c + 1) * CHUNK, hs] for c in range(n_sub)], axis=1)
                gate.append(_bdot(ws[h0 + h], vcat))
            s = jnp.concatenate(
                [jnp.concatenate([gate[h][:, c * CHUNK:(c + 1) * CHUNK] + bias[h0 + h]
                                  for h in range(sub_heads)], axis=1)
                 for c in range(n_sub)], axis=0)
            yb = gu * s
            y_ref[r0:r0 + rc, c0:c0 + cc] = (yb * g[:, c0:c0 + cc]).astype(y_ref.dtype)
            part = _lane_group_sum(yb * yb)
            sq = part if c0 == 0 else sq + part
            if c0 + cc == tn:
                ssq_ref[0, r0:r0 + rc, :] = sq

    _run_streamed_pass(step, wbf_a, wbf_b)


def _gmlp_mixer(xn, w_in, spatial_w, spatial_b, g_b, col0, gw, w_out, tm=1024, tn=512,
                rcs=(256, 256, 256, 128, 128), cc=256):
    t, d = xn.shape
    nj, ni = gw // tn, t // tm
    assert sum(rcs) == tm and all(rc % CHUNK == 0 for rc in rcs)
    assert tn % cc == 0 and cc % HEAD_DIM == 0
    off_u = col0 // tn
    sb = jnp.broadcast_to(spatial_b[:, :, None], spatial_b.shape + (HEAD_DIM,))
    kern = functools.partial(_gmlp_kernel, rcs, cc)
    next_col = lambda p: jnp.minimum(p, nj - 1)
    wo_spec = _pass_slab_spec(w_out.shape[0], w_out.shape[1], nj, ni)
    return pl.pallas_call(
        kern,
        grid=(nj + 1, ni),
        in_specs=[pl.BlockSpec((tm, d), lambda p, i: (_pass_row_tile(p, i), 0)),
                  pl.BlockSpec((d // ni, tn), lambda p, i: (i, off_u + next_col(p))),
                  pl.BlockSpec((d // ni, tn), lambda p, i: (i, off_u + nj + next_col(p))),
                  _WHOLE_VMEM, _WHOLE_VMEM, _WHOLE_VMEM,
                  wo_spec],
        out_specs=[pl.BlockSpec((tm, tn),
                                lambda p, i: (_pass_row_tile(p, i), _pass_col_tile(p))),
                   _ssq_spec(tm, _pass_row_tile, _pass_col_tile),
                   wo_spec],
        out_shape=[jax.ShapeDtypeStruct((t, gw), jnp.bfloat16),
                   jax.ShapeDtypeStruct((nj, t, LANES), jnp.float32),
                   jax.ShapeDtypeStruct(w_out.shape, jnp.bfloat16)],
        scratch_shapes=[pltpu.VMEM((2, d, tn), jnp.bfloat16),
                        pltpu.VMEM((2, d, tn), jnp.bfloat16)],
        compiler_params=_params(_ARB2),
        name="gmlp_mixer",
    )(xn, w_in, w_in, spatial_w, sb, g_b.reshape(nj, tn), w_out)


def _out_proj_kernel(ccs, ya_ref, yb_ref, wa_ref, wb_ref, sa_ref, sb_ref, x_ref, g_ref,
                     h_ref, xg_ref, ssq_ref):
    ya, yb = ya_ref[...], yb_ref[...]
    inv_a = _inv_rms(sa_ref, ya.shape[1])
    inv_b = _inv_rms(sb_ref, yb.shape[1])
    g = g_ref[pl.ds(pl.program_id(0), 1), :]
    sq = None
    for c0, cc in _chunks(ccs):
        cs = slice(c0, c0 + cc)
        h = x_ref[:, cs] + _bdot(ya, wa_ref[:, cs]) * inv_a + _bdot(yb, wb_ref[:, cs]) * inv_b
        h_ref[:, cs] = h
        xg_ref[:, cs] = (h * g[:, cs]).astype(xg_ref.dtype)
        part = _lane_group_sum(h * h)
        sq = part if sq is None else sq + part
    ssq_ref[0] = sq


def _out_proj(ya, yb, w_out, ssq_a, ssq_b, x, g_mlp, tm=512, tn=2048,
              ccs=(512, 512, 512, 256, 256)):
    t, cw = ya.shape
    gw = yb.shape[1]
    d = w_out.shape[1]
    assert cw == gw and sum(ccs) == tn
    kern = functools.partial(_out_proj_kernel, ccs)
    ssq_in = lambda a: pl.BlockSpec((a.shape[0], tm, LANES), lambda j, i: (0, i, 0))
    tile_spec = pl.BlockSpec((tm, tn), lambda j, i: (i, j))
    return pl.pallas_call(
        kern,
        grid=(d // tn, t // tm),
        in_specs=[pl.BlockSpec((tm, cw), lambda j, i: (i, 0)),
                  pl.BlockSpec((tm, gw), lambda j, i: (i, 0)),
                  pl.BlockSpec((cw, tn), lambda j, i: (0, j), pipeline_mode=pl.Buffered(1)),
                  pl.BlockSpec((gw, tn), lambda j, i: (1, j), pipeline_mode=pl.Buffered(1)),
                  ssq_in(ssq_a), ssq_in(ssq_b), tile_spec, _WHOLE_VMEM],
        out_specs=[tile_spec, tile_spec,
                   _ssq_spec(tm, lambda j, i: i, lambda j: j)],
        out_shape=[jax.ShapeDtypeStruct((t, d), jnp.float32),
                   jax.ShapeDtypeStruct((t, d), jnp.bfloat16),
                   jax.ShapeDtypeStruct((d // tn, t, LANES), jnp.float32)],
        compiler_params=_params(_ARB2),
        name="out_proj",
    )(ya, yb, w_out, w_out, ssq_a, ssq_b, x, g_mlp.reshape(d // tn, tn))


def _mlp_up_kernel(rc, x_ref, wc_ref, wd_ref, r_ref, wdb_ref, wbf_a, wbf_b):
    kc = wc_ref.shape[0]
    chunk_rows = pl.ds(pl.multiple_of(pl.program_id(1) * kc, kc), kc)
    wdb_ref[...] = wd_ref[...].astype(wdb_ref.dtype)

    def step(cur_ref, nxt_ref):
        nxt_ref[chunk_rows, :] = wc_ref[...].astype(nxt_ref.dtype)
        if cur_ref is None:
            return
        for r in range(r_ref.shape[0] // rc):
            rs = slice(r * rc, (r + 1) * rc)
            z = jnp.maximum(_bdot(x_ref[rs, :], cur_ref[...]), 0.0)
            r_ref[rs, :] = (z * z).astype(r_ref.dtype)

    _run_streamed_pass(step, wbf_a, wbf_b)


def _mlp_up(xg, w_up, w_down, tm=1024, tn=1024, rc=1024):
    t, d = xg.shape
    f = w_up.shape[1]
    nj, ni = f // tn, t // tm
    kern = functools.partial(_mlp_up_kernel, rc)
    slab_spec = _pass_slab_spec(f, d, nj, ni)
    return pl.pallas_call(
        kern,
        grid=(nj + 1, ni),
        in_specs=[pl.BlockSpec((tm, d), lambda p, i: (_pass_row_tile(p, i), 0)),
                  pl.BlockSpec((d // ni, tn), lambda p, i: (i, jnp.minimum(p, nj - 1))),
                  slab_spec],
        out_specs=[pl.BlockSpec((tm, tn),
                                lambda p, i: (_pass_row_tile(p, i), _pass_col_tile(p))),
                   slab_spec],
        out_shape=[jax.ShapeDtypeStruct((t, f), jnp.bfloat16),
                   jax.ShapeDtypeStruct(w_down.shape, jnp.bfloat16)],
        scratch_shapes=[pltpu.VMEM((d, tn), jnp.bfloat16),
                        pltpu.VMEM((d, tn), jnp.bfloat16)],
        compiler_params=_params(_ARB2),
        name="mlp_up",
    )(xg, w_up, w_down)


def _mlp_down_kernel(width, r_ref, w_ref, h_ref, ssq_ref, o_ref):
    k = pl.program_id(2)
    last = pl.num_programs(2) - 1

    @pl.when(k == 0)
    def _():
        o_ref[...] = _bdot(r_ref[...], w_ref[...])

    @pl.when((k > 0) & (k < last))
    def _():
        o_ref[...] += _bdot(r_ref[...], w_ref[...])

    @pl.when(k == last)
    def _():
        inv = _inv_rms(ssq_ref, width)
        o_ref[...] = h_ref[...] + inv * inv * (o_ref[...] + _bdot(r_ref[...], w_ref[...]))


def _mlp_down(r, w_down, h, ssq, tm=1024, tn=1024, tk=4096):
    t, f = r.shape
    d = w_down.shape[1]
    assert f // tk >= 2
    return pl.pallas_call(
        functools.partial(_mlp_down_kernel, d),
        grid=(d // tn, t // tm, f // tk),
        in_specs=[pl.BlockSpec((tm, tk), lambda j, i, k: (i, k)),
                  pl.BlockSpec((tk, tn), lambda j, i, k: (k, j)),
                  pl.BlockSpec((tm, tn), lambda j, i, k: (i, j)),
                  pl.BlockSpec((ssq.shape[0], tm, LANES), lambda j, i, k: (0, i, 0))],
        out_specs=pl.BlockSpec((tm, tn), lambda j, i, k: (i, j)),
        out_shape=jax.ShapeDtypeStruct((t, d), jnp.float32),
        compiler_params=_params(("arbitrary", "arbitrary", "arbitrary")),
        name="mlp_down",
    )(r, w_down, h, ssq)


def kernel(x, mix_norm_g, w_in, conv_w, spatial_w, spatial_b, conv_out_norm_g,
           gmlp_out_norm_g, w_out, mlp_norm_g, w_up, w_down, final_norm_g):
    bsz, seq, d = x.shape
    depth = w_in.shape[0]
    cw = conv_w.shape[2]
    gw = spatial_w.shape[1] * HEAD_DIM
    assert seq % CHUNK == 0 and w_in.shape[2] == 3 * cw + 2 * gw
    h = x.reshape(bsz * seq, d)
    for l in range(depth):
        xn = _rmsnorm(h, mix_norm_g[l], jnp.bfloat16)
        ya, ssq_a = _conv_mixer(xn, w_in[l], conv_w[l], conv_out_norm_g[l], seq, cw)
        yb, ssq_b, w_out_bf16 = _gmlp_mixer(xn, w_in[l], spatial_w[l], spatial_b[l],
                                            gmlp_out_norm_g[l], 3 * cw, gw, w_out[l])
        h, xg, ssq_h = _out_proj(ya, yb, w_out_bf16, ssq_a, ssq_b, h, mlp_norm_g[l])
        r, w_down_bf16 = _mlp_up(xg, w_up[l], w_down[l])
        h = _mlp_down(r, w_down_bf16, h, ssq_h)
    out = _rmsnorm(h, final_norm_g, x.dtype)
    return out.reshape(bsz, seq, d)
```

```python
import functools

import jax
import jax.numpy as jnp
from jax import lax
from jax.experimental import pallas as pl
from jax.experimental.pallas import tpu as pltpu

EPS = 1e-5
HEAD_DIM = 128
CHUNK = 128
CONV_K = 3
LANES = 128
HALO = 8

_VMEM_LIMIT = 60000 * 1024

_ARB2 = ("arbitrary", "arbitrary")


def _params(sem):
    return pltpu.CompilerParams(dimension_semantics=sem, vmem_limit_bytes=_VMEM_LIMIT)


def _bdot(a, b):
    return jnp.dot(a.astype(jnp.bfloat16), b.astype(jnp.bfloat16),
                   preferred_element_type=jnp.float32)


def _lane_group_sum(sq):
    acc = sq[:, 0:LANES]
    for c in range(1, sq.shape[1] // LANES):
        acc = acc + sq[:, c * LANES:(c + 1) * LANES]
    return acc


def _ssq_spec(tm, row_tile, col_tile):
    return pl.BlockSpec((1, tm, LANES), lambda a, i: (col_tile(a), row_tile(a, i), 0))


def _inv_rms(ssq_ref, width):
    total = jnp.sum(jnp.sum(ssq_ref[...], axis=0), axis=-1, keepdims=True)
    return lax.rsqrt(total / width + EPS)


_WHOLE_VMEM = pl.BlockSpec(memory_space=pltpu.VMEM)


def _chunks(sizes):
    out, start = [], 0
    for size in sizes:
        out.append((start, size))
        start += size
    return out


def _pass_row_tile(p, i):
    return jnp.where(p == 0, 0, i)


def _pass_col_tile(p):
    return jnp.maximum(p - 1, 0)


def _pass_slab_spec(rows, cols, nj, ni):
    return pl.BlockSpec((rows // (nj * ni), cols),
                        lambda p, i: (_pass_col_tile(p) * ni + _pass_row_tile(p, i), 0))


def _run_streamed_pass(step, wbf_a, wbf_b):
    p = pl.program_id(0)
    pl.when(p == 0)(lambda: step(None, wbf_a))
    pl.when((p > 0) & (p % 2 == 1))(lambda: step(wbf_a, wbf_b))
    pl.when((p > 0) & (p % 2 == 0))(lambda: step(wbf_b, wbf_a))


def _rmsnorm_kernel(x_ref, g_ref, o_ref):
    x = x_ref[...]
    ms = jnp.mean(x * x, axis=-1, keepdims=True)
    o_ref[...] = (x * lax.rsqrt(ms + EPS) * g_ref[...]).astype(o_ref.dtype)


def _rmsnorm(x, g, out_dtype, tr=512):
    t, d = x.shape
    return pl.pallas_call(
        _rmsnorm_kernel,
        grid=(t // tr,),
        in_specs=[pl.BlockSpec((tr, d), lambda i: (i, 0)),
                  pl.BlockSpec((1, d), lambda i: (0, 0))],
        out_specs=pl.BlockSpec((tr, d), lambda i: (i, 0)),
        out_shape=jax.ShapeDtypeStruct((t, d), out_dtype),
        compiler_params=_params(("arbitrary",)),
        name="rmsnorm_cast",
    )(x, g.reshape(1, d))


def _conv_kernel(tiles_per_seq, rcs, x_ref, wb_ref, wc_ref, wh_ref, cw_ref, g_ref,
                 y_ref, ssq_ref, wbf_ref, ext_ref):
    j, i = pl.program_id(0), pl.program_id(1)
    tm, tn = y_ref.shape
    taps = [cw_ref[k, pl.ds(j, 1), :] for k in range(CONV_K)]
    g = g_ref[pl.ds(j, 1), :]

    @pl.when(i == 0)
    def _():
        for n, w_ref in enumerate((wb_ref, wc_ref, wh_ref)):
            wbf_ref[n] = w_ref[...].astype(wbf_ref.dtype)

    @pl.when(i % tiles_per_seq == 0)
    def _():
        ext_ref[0:HALO, :] = jnp.zeros((HALO, tn), jnp.float32)

    for r0, rc in _chunks(rcs):
        rs = slice(r0, r0 + rc)
        xb = x_ref[rs, :]
        bg = _bdot(xb, wbf_ref[0])
        ch = _bdot(xb, wbf_ref[1]) * _bdot(xb, wbf_ref[2])
        base = HALO + r0
        ext_ref[base:base + rc, :] = ch
        conv = taps[CONV_K - 1] * ch
        for k in range(CONV_K - 1):
            shift = CONV_K - 1 - k
            conv = conv + taps[k] * ext_ref[base - shift:base - shift + rc, :]
        ya = bg * conv
        y_ref[rs, :] = (ya * g).astype(y_ref.dtype)
        ssq_ref[0, rs, :] = _lane_group_sum(ya * ya)
    ext_ref[0:HALO, :] = ext_ref[tm:tm + HALO, :]


def _conv_mixer(xn, w_in, conv_w, g_a, seq, cw, tm=1024, tn=256,
                rcs=(256, 256, 256, 128, 128)):
    t, d = xn.shape
    nj = cw // tn
    assert sum(rcs) == tm
    kern = functools.partial(_conv_kernel, seq // tm, rcs)
    wspec = lambda off: pl.BlockSpec((d, tn), lambda j, i, off=off: (0, j + off))
    return pl.pallas_call(
        kern,
        grid=(nj, t // tm),
        in_specs=[pl.BlockSpec((tm, d), lambda j, i: (i, 0)),
                  wspec(0), wspec(nj), wspec(2 * nj), _WHOLE_VMEM, _WHOLE_VMEM],
        out_specs=[pl.BlockSpec((tm, tn), lambda j, i: (i, j)),
                   _ssq_spec(tm, lambda j, i: i, lambda j: j)],
        out_shape=[jax.ShapeDtypeStruct((t, cw), jnp.bfloat16),
                   jax.ShapeDtypeStruct((nj, t, LANES), jnp.float32)],
        scratch_shapes=[pltpu.VMEM((3, d, tn), jnp.bfloat16),
                        pltpu.VMEM((HALO + tm, tn), jnp.float32)],
        compiler_params=_params(_ARB2),
        name="conv_mixer",
    )(xn, w_in, w_in, w_in, conv_w.reshape(CONV_K, nj, tn), g_a.reshape(nj, tn))


def _gmlp_kernel(rcs, cc, x_ref, wuc_ref, wvc_ref, sw_ref, sb_ref, g_ref, wo_ref,
                 y_ref, ssq_ref, wob_ref, wbf_a, wbf_b):
    tm, tn = y_ref.shape
    kc = wuc_ref.shape[0]
    chunk_rows = pl.ds(pl.multiple_of(pl.program_id(1) * kc, kc), kc)
    wob_ref[...] = wo_ref[...].astype(wob_ref.dtype)

    def step(cur_ref, nxt_ref):
        nxt_ref[0, chunk_rows, :] = wuc_ref[...].astype(nxt_ref.dtype)
        nxt_ref[1, chunk_rows, :] = wvc_ref[...].astype(nxt_ref.dtype)
        if cur_ref is None:
            return
        col_tile = pl.program_id(0) - 1
        row = lax.broadcasted_iota(jnp.int32, (CHUNK, CHUNK), 0)
        col = lax.broadcasted_iota(jnp.int32, (CHUNK, CHUNK), 1)
        causal = col <= row
        heads = tn // HEAD_DIM
        head0 = col_tile * heads
        ws = [jnp.where(causal, sw_ref[head0 + h], 0.0).astype(jnp.bfloat16)
              for h in range(heads)]
        bias = [sb_ref[head0 + h] for h in range(heads)]
        g = g_ref[pl.ds(col_tile, 1), :]
        units = [(r0, rc, c0) for r0, rc in _chunks(rcs) for c0 in range(0, tn, cc)]
        sub_heads = cc // HEAD_DIM

        def uv_dots(r0, rc, c0):
            xb = x_ref[r0:r0 + rc, :]
            return (_bdot(xb, cur_ref[0, :, c0:c0 + cc]),
                    _bdot(xb, cur_ref[1, :, c0:c0 + cc]))

        uv = uv_dots(*units[0])
        sq = None
        for n, (r0, rc, c0) in enumerate(units):
            u, v = uv
            if n + 1 < len(units):
                uv = uv_dots(*units[n + 1])
            gu = jax.nn.gelu(u)
            gv = jax.nn.gelu(v).astype(jnp.bfloat16)
            n_sub = rc // CHUNK
            h0 = c0 // HEAD_DIM
            gate = []
            for h in range(sub_heads):
                hs = slice(h * HEAD_DIM, (h + 1) * HEAD_DIM)
                vcat = jnp.concatenate(
                    [gv[c * CHUNK:(c + 1) * CHUNK, hs] for c in range(n_sub)], axis=1)
                gate.append(_bdot(ws[h0 + h], vcat))
            s = jnp.concatenate(
                [jnp.concatenate([gate[h][:, c * CHUNK:(c + 1) * CHUNK] + bias[h0 + h]
                                  for h in range(sub_heads)], axis=1)
                 for c in range(n_sub)], axis=0)
            yb = gu * s
            y_ref[r0:r0 + rc, c0:c0 + cc] = (yb * g[:, c0:c0 + cc]).astype(y_ref.dtype)
            part = _lane_group_sum(yb * yb)
            sq = part if c0 == 0 else sq + part
            if c0 + cc == tn:
                ssq_ref[0, r0:r0 + rc, :] = sq

    _run_streamed_pass(step, wbf_a, wbf_b)


def _gmlp_mixer(xn, w_in, spatial_w, spatial_b, g_b, col0, gw, w_out, tm=1024, tn=512,
                rcs=(256, 256, 256, 128, 128), cc=256):
    t, d = xn.shape
    nj, ni = gw // tn, t // tm
    assert sum(rcs) == tm and all(rc % CHUNK == 0 for rc in rcs)
    assert tn % cc == 0 and cc % HEAD_DIM == 0
    off_u = col0 // tn
    sb = jnp.broadcast_to(spatial_b[:, :, None], spatial_b.shape + (HEAD_DIM,))
    kern = functools.partial(_gmlp_kernel, rcs, cc)
    next_col = lambda p: jnp.minimum(p, nj - 1)
    wo_spec = _pass_slab_spec(w_out.shape[0], w_out.shape[1], nj, ni)
    return pl.pallas_call(
        kern,
        grid=(nj + 1, ni),
        in_specs=[pl.BlockSpec((tm, d), lambda p, i: (_pass_row_tile(p, i), 0)),
                  pl.BlockSpec((d // ni, tn), lambda p, i: (i, off_u + next_col(p))),
                  pl.BlockSpec((d // ni, tn), lambda p, i: (i, off_u + nj + next_col(p))),
                  _WHOLE_VMEM, _WHOLE_VMEM, _WHOLE_VMEM,
                  wo_spec],
        out_specs=[pl.BlockSpec((tm, tn),
                                lambda p, i: (_pass_row_tile(p, i), _pass_col_tile(p))),
                   _ssq_spec(tm, _pass_row_tile, _pass_col_tile),
                   wo_spec],
        out_shape=[jax.ShapeDtypeStruct((t, gw), jnp.bfloat16),
                   jax.ShapeDtypeStruct((nj, t, LANES), jnp.float32),
                   jax.ShapeDtypeStruct(w_out.shape, jnp.bfloat16)],
        scratch_shapes=[pltpu.VMEM((2, d, tn), jnp.bfloat16),
                        pltpu.VMEM((2, d, tn), jnp.bfloat16)],
        compiler_params=_params(_ARB2),
        name="gmlp_mixer",
    )(xn, w_in, w_in, spatial_w, sb, g_b.reshape(nj, tn), w_out)


def _out_proj_kernel(ccs, ya_ref, yb_ref, wa_ref, wb_ref, sa_ref, sb_ref, x_ref, g_ref,
                     h_ref, xg_ref, ssq_ref):
    ya, yb = ya_ref[...], yb_ref[...]
    inv_a = _inv_rms(sa_ref, ya.shape[1])
    inv_b = _inv_rms(sb_ref, yb.shape[1])
    g = g_ref[pl.ds(pl.program_id(0), 1), :]
    sq = None
    for c0, cc in _chunks(ccs):
        cs = slice(c0, c0 + cc)
        h = x_ref[:, cs] + _bdot(ya, wa_ref[:, cs]) * inv_a + _bdot(yb, wb_ref[:, cs]) * inv_b
        h_ref[:, cs] = h
        xg_ref[:, cs] = (h * g[:, cs]).astype(xg_ref.dtype)
        part = _lane_group_sum(h * h)
        sq = part if sq is None else sq + part
    ssq_ref[0] = sq


def _out_proj(ya, yb, w_out, ssq_a, ssq_b, x, g_mlp, tm=512, tn=2048,
              ccs=(512, 512, 512, 256, 256)):
    t, cw = ya.shape
    gw = yb.shape[1]
    d = w_out.shape[1]
    assert cw == gw and sum(ccs) == tn
    kern = functools.partial(_out_proj_kernel, ccs)
    ssq_in = lambda a: pl.BlockSpec((a.shape[0], tm, LANES), lambda j, i: (0, i, 0))
    tile_spec = pl.BlockSpec((tm, tn), lambda j, i: (i, j))
    return pl.pallas_call(
        kern,
        grid=(d // tn, t // tm),
        in_specs=[pl.BlockSpec((tm, cw), lambda j, i: (i, 0)),
                  pl.BlockSpec((tm, gw), lambda j, i: (i, 0)),
                  pl.BlockSpec((cw, tn), lambda j, i: (0, j), pipeline_mode=pl.Buffered(1)),
                  pl.BlockSpec((gw, tn), lambda j, i: (1, j), pipeline_mode=pl.Buffered(1)),
                  ssq_in(ssq_a), ssq_in(ssq_b), tile_spec, _WHOLE_VMEM],
        out_specs=[tile_spec, tile_spec,
                   _ssq_spec(tm, lambda j, i: i, lambda j: j)],
        out_shape=[jax.ShapeDtypeStruct((t, d), jnp.float32),
                   jax.ShapeDtypeStruct((t, d), jnp.bfloat16),
                   jax.ShapeDtypeStruct((d // tn, t, LANES), jnp.float32)],
        compiler_params=_params(_ARB2),
        name="out_proj",
    )(ya, yb, w_out, w_out, ssq_a, ssq_b, x, g_mlp.reshape(d // tn, tn))


def _mlp_up_kernel(rc, x_ref, wc_ref, wd_ref, r_ref, wdb_ref, wbf_a, wbf_b):
    kc = wc_ref.shape[0]
    chunk_rows = pl.ds(pl.multiple_of(pl.program_id(1) * kc, kc), kc)
    wdb_ref[...] = wd_ref[...].astype(wdb_ref.dtype)

    def step(cur_ref, nxt_ref):
        nxt_ref[chunk_rows, :] = wc_ref[...].astype(nxt_ref.dtype)
        if cur_ref is None:
            return
        for r in range(r_ref.shape[0] // rc):
            rs = slice(r * rc, (r + 1) * rc)
            z = jnp.maximum(_bdot(x_ref[rs, :], cur_ref[...]), 0.0)
            r_ref[rs, :] = (z * z).astype(r_ref.dtype)

    _run_streamed_pass(step, wbf_a, wbf_b)


def _mlp_up(xg, w_up, w_down, tm=1024, tn=1024, rc=1024):
    t, d = xg.shape
    f = w_up.shape[1]
    nj, ni = f // tn, t // tm
    kern = functools.partial(_mlp_up_kernel, rc)
    slab_spec = _pass_slab_spec(f, d, nj, ni)
    return pl.pallas_call(
        kern,
        grid=(nj + 1, ni),
        in_specs=[pl.BlockSpec((tm, d), lambda p, i: (_pass_row_tile(p, i), 0)),
                  pl.BlockSpec((d // ni, tn), lambda p, i: (i, jnp.minimum(p, nj - 1))),
                  slab_spec],
        out_specs=[pl.BlockSpec((tm, tn),
                                lambda p, i: (_pass_row_tile(p, i), _pass_col_tile(p))),
                   slab_spec],
        out_shape=[jax.ShapeDtypeStruct((t, f), jnp.bfloat16),
                   jax.ShapeDtypeStruct(w_down.shape, jnp.bfloat16)],
        scratch_shapes=[pltpu.VMEM((d, tn), jnp.bfloat16),
                        pltpu.VMEM((d, tn), jnp.bfloat16)],
        compiler_params=_params(_ARB2),
        name="mlp_up",
    )(xg, w_up, w_down)


def _mlp_down_kernel(width, r_ref, w_ref, h_ref, ssq_ref, o_ref):
    k = pl.program_id(2)
    last = pl.num_programs(2) - 1

    @pl.when(k == 0)
    def _():
        o_ref[...] = _bdot(r_ref[...], w_ref[...])

    @pl.when((k > 0) & (k < last))
    def _():
        o_ref[...] += _bdot(r_ref[...], w_ref[...])

    @pl.when(k == last)
    def _():
        inv = _inv_rms(ssq_ref, width)
        o_ref[...] = h_ref[...] + inv * inv * (o_ref[...] + _bdot(r_ref[...], w_ref[...]))


def _mlp_down(r, w_down, h, ssq, tm=1024, tn=1024, tk=4096):
    t, f = r.shape
    d = w_down.shape[1]
    assert f // tk >= 2
    return pl.pallas_call(
        functools.partial(_mlp_down_kernel, d),
        grid=(d // tn, t // tm, f // tk),
        in_specs=[pl.BlockSpec((tm, tk), lambda j, i, k: (i, k)),
                  pl.BlockSpec((tk, tn), lambda j, i, k: (k, j)),
                  pl.BlockSpec((tm, tn), lambda j, i, k: (i, j)),
                  pl.BlockSpec((ssq.shape[0], tm, LANES), lambda j, i, k: (0, i, 0))],
        out_specs=pl.BlockSpec((tm, tn), lambda j, i, k: (i, j)),
        out_shape=jax.ShapeDtypeStruct((t, d), jnp.float32),
        compiler_params=_params(("arbitrary", "arbitrary", "arbitrary")),
        name="mlp_down",
    )(r, w_down, h, ssq)


def kernel(x, mix_norm_g, w_in, conv_w, spatial_w, spatial_b, conv_out_norm_g,
           gmlp_out_norm_g, w_out, mlp_norm_g, w_up, w_down, final_norm_g):
    bsz, seq, d = x.shape
    depth = w_in.shape[0]
    cw = conv_w.shape[2]
    gw = spatial_w.shape[1] * HEAD_DIM
    assert seq % CHUNK == 0 and w_in.shape[2] == 3 * cw + 2 * gw
    h = x.reshape(bsz * seq, d)
    for l in range(depth):
        xn = _rmsnorm(h, mix_norm_g[l], jnp.bfloat16)
        ya, ssq_a = _conv_mixer(xn, w_in[l], conv_w[l], conv_out_norm_g[l], seq, cw)
        yb, ssq_b, w_out_bf16 = _gmlp_mixer(xn, w_in[l], spatial_w[l], spatial_b[l],
                                            gmlp_out_norm_g[l], 3 * cw, gw, w_out[l])
        h, xg, ssq_h = _out_proj(ya, yb, w_out_bf16, ssq_a, ssq_b, h, mlp_norm_g[l])
        r, w_down_bf16 = _mlp_up(xg, w_up[l], w_down[l])
        h = _mlp_down(r, w_down_bf16, h, ssq_h)
    out = _rmsnorm(h, final_norm_g, x.dtype)
    return out.reshape(bsz, seq, d)
```

```python
import functools

import jax
import jax.numpy as jnp
from jax import lax
from jax.experimental import pallas as pl
from jax.experimental.pallas import tpu as pltpu

EPS = 1e-5
HEAD_DIM = 128
CHUNK = 128
CONV_K = 3
LANES = 128
HALO = 8

_VMEM_LIMIT = 60000 * 1024

_ARB2 = ("arbitrary", "arbitrary")


def _params(sem):
    return pltpu.CompilerParams(dimension_semantics=sem, vmem_limit_bytes=_VMEM_LIMIT)


def _bdot(a, b):
    return jnp.dot(a.astype(jnp.bfloat16), b.astype(jnp.bfloat16),
                   preferred_element_type=jnp.float32)


def _lane_group_sum(sq):
    acc = sq[:, 0:LANES]
    for c in range(1, sq.shape[1] // LANES):
        acc = acc + sq[:, c * LANES:(c + 1) * LANES]
    return acc


def _ssq_spec(tm, row_tile, col_tile):
    return pl.BlockSpec((1, tm, LANES), lambda a, i: (col_tile(a), row_tile(a, i), 0))


def _inv_rms(ssq_ref, width):
    total = jnp.sum(jnp.sum(ssq_ref[...], axis=0), axis=-1, keepdims=True)
    return lax.rsqrt(total / width + EPS)


_WHOLE_VMEM = pl.BlockSpec(memory_space=pltpu.VMEM)


def _chunks(sizes):
    out, start = [], 0
    for size in sizes:
        out.append((start, size))
        start += size
    return out


def _pass_row_tile(p, i):
    return jnp.where(p == 0, 0, i)


def _pass_col_tile(p):
    return jnp.maximum(p - 1, 0)


def _pass_slab_spec(rows, cols, nj, ni):
    return pl.BlockSpec((rows // (nj * ni), cols),
                        lambda p, i: (_pass_col_tile(p) * ni + _pass_row_tile(p, i), 0))


def _run_streamed_pass(step, wbf_a, wbf_b):
    p = pl.program_id(0)
    pl.when(p == 0)(lambda: step(None, wbf_a))
    pl.when((p > 0) & (p % 2 == 1))(lambda: step(wbf_a, wbf_b))
    pl.when((p > 0) & (p % 2 == 0))(lambda: step(wbf_b, wbf_a))


def _rmsnorm_kernel(x_ref, g_ref, o_ref):
    x = x_ref[...]
    ms = jnp.mean(x * x, axis=-1, keepdims=True)
    o_ref[...] = (x * lax.rsqrt(ms + EPS) * g_ref[...]).astype(o_ref.dtype)


def _rmsnorm(x, g, out_dtype, tr=512):
    t, d = x.shape
    return pl.pallas_call(
        _rmsnorm_kernel,
        grid=(t // tr,),
        in_specs=[pl.BlockSpec((tr, d), lambda i: (i, 0)),
                  pl.BlockSpec((1, d), lambda i: (0, 0))],
        out_specs=pl.BlockSpec((tr, d), lambda i: (i, 0)),
        out_shape=jax.ShapeDtypeStruct((t, d), out_dtype),
        compiler_params=_params(("arbitrary",)),
        name="rmsnorm_cast",
    )(x, g.reshape(1, d))


def _conv_kernel(tiles_per_seq, rcs, x_ref, wb_ref, wc_ref, wh_ref, cw_ref, g_ref,
                 y_ref, ssq_ref, wbf_ref, ext_ref):
    j, i = pl.program_id(0), pl.program_id(1)
    tm, tn = y_ref.shape
    taps = [cw_ref[k, pl.ds(j, 1), :] for k in range(CONV_K)]
    g = g_ref[pl.ds(j, 1), :]

    @pl.when(i == 0)
    def _():
        for n, w_ref in enumerate((wb_ref, wc_ref, wh_ref)):
            wbf_ref[n] = w_ref[...].astype(wbf_ref.dtype)

    @pl.when(i % tiles_per_seq == 0)
    def _():
        ext_ref[0:HALO, :] = jnp.zeros((HALO, tn), jnp.float32)

    for r0, rc in _chunks(rcs):
        rs = slice(r0, r0 + rc)
        xb = x_ref[rs, :]
        bg = _bdot(xb, wbf_ref[0])
        ch = _bdot(xb, wbf_ref[1]) * _bdot(xb, wbf_ref[2])
        base = HALO + r0
        ext_ref[base:base + rc, :] = ch
        conv = taps[CONV_K - 1] * ch
        for k in range(CONV_K - 1):
            shift = CONV_K - 1 - k
            conv = conv + taps[k] * ext_ref[base - shift:base - shift + rc, :]
        ya = bg * conv
        y_ref[rs, :] = (ya * g).astype(y_ref.dtype)
        ssq_ref[0, rs, :] = _lane_group_sum(ya * ya)
    ext_ref[0:HALO, :] = ext_ref[tm:tm + HALO, :]


def _conv_mixer(xn, w_in, conv_w, g_a, seq, cw, tm=1024, tn=256,
                rcs=(256, 256, 256, 128, 128)):
    t, d = xn.shape
    nj = cw // tn
    assert sum(rcs) == tm
    kern = functools.partial(_conv_kernel, seq // tm, rcs)
    wspec = lambda off: pl.BlockSpec((d, tn), lambda j, i, off=off: (0, j + off))
    return pl.pallas_call(
        kern,
        grid=(nj, t // tm),
        in_specs=[pl.BlockSpec((tm, d), lambda j, i: (i, 0)),
                  wspec(0), wspec(nj), wspec(2 * nj), _WHOLE_VMEM, _WHOLE_VMEM],
        out_specs=[pl.BlockSpec((tm, tn), lambda j, i: (i, j)),
                   _ssq_spec(tm, lambda j, i: i, lambda j: j)],
        out_shape=[jax.ShapeDtypeStruct((t, cw), jnp.bfloat16),
                   jax.ShapeDtypeStruct((nj, t, LANES), jnp.float32)],
        scratch_shapes=[pltpu.VMEM((3, d, tn), jnp.bfloat16),
                        pltpu.VMEM((HALO + tm, tn), jnp.float32)],
        compiler_params=_params(_ARB2),
        name="conv_mixer",
    )(xn, w_in, w_in, w_in, conv_w.reshape(CONV_K, nj, tn), g_a.reshape(nj, tn))


def _gmlp_kernel(rcs, cc, x_ref, wuc_ref, wvc_ref, sw_ref, sb_ref, g_ref, wo_ref,
                 y_ref, ssq_ref, wob_ref, wbf_a, wbf_b):
    tm, tn = y_ref.shape
    kc = wuc_ref.shape[0]
    chunk_rows = pl.ds(pl.multiple_of(pl.program_id(1) * kc, kc), kc)
    wob_ref[...] = wo_ref[...].astype(wob_ref.dtype)

    def step(cur_ref, nxt_ref):
        nxt_ref[0, chunk_rows, :] = wuc_ref[...].astype(nxt_ref.dtype)
        nxt_ref[1, chunk_rows, :] = wvc_ref[...].astype(nxt_ref.dtype)
        if cur_ref is None:
            return
        col_tile = pl.program_id(0) - 1
        row = lax.broadcasted_iota(jnp.int32, (CHUNK, CHUNK), 0)
        col = lax.broadcasted_iota(jnp.int32, (CHUNK, CHUNK), 1)
        causal = col <= row
        heads = tn // HEAD_DIM
        head0 = col_tile * heads
        ws = [jnp.where(causal, sw_ref[head0 + h], 0.0).astype(jnp.bfloat16)
              for h in range(heads)]
        bias = [sb_ref[head0 + h] for h in range(heads)]
        g = g_ref[pl.ds(col_tile, 1), :]
        units = [(r0, rc, c0) for r0, rc in _chunks(rcs) for c0 in range(0, tn, cc)]
        sub_heads = cc // HEAD_DIM

        def uv_dots(r0, rc, c0):
            xb = x_ref[r0:r0 + rc, :]
            return (_bdot(xb, cur_ref[0, :, c0:c0 + cc]),
                    _bdot(xb, cur_ref[1, :, c0:c0 + cc]))

        uv = uv_dots(*units[0])
        sq = None
        for n, (r0, rc, c0) in enumerate(units):
            u, v = uv
            if n + 1 < len(units):
                uv = uv_dots(*units[n + 1])
            gu = jax.nn.gelu(u)
            gv = jax.nn.gelu(v).astype(jnp.bfloat16)
            n_sub = rc // CHUNK
            h0 = c0 // HEAD_DIM
            gate = []
            for h in range(sub_heads):
                hs = slice(h * HEAD_DIM, (h + 1) * HEAD_DIM)
                vcat = jnp.concatenate(
                    [gv[c * CHUNK:(c + 1) * CHUNK, hs] for c in range(n_sub)], axis=1)
                gate.append(_bdot(ws[h0 + h], vcat))
            s = jnp.concatenate(
                [jnp.concatenate([gate[h][:, c * CHUNK:(c + 1) * CHUNK] + bias[h0 + h]
                                  for h in range(sub_heads)], axis=1)
                 for c in range(n_sub)], axis=0)
            yb = gu * s
            y_ref[r0:r0 + rc, c0:c0 + cc] = (yb * g[:, c0:c0 + cc]).astype(y_ref.dtype)
            part = _lane_group_sum(yb * yb)
            sq = part if c0 == 0 else sq + part
            if c0 + cc == tn:
                ssq_ref[0, r0:r0 + rc, :] = sq

    _run_streamed_pass(step, wbf_a, wbf_b)


def _gmlp_mixer(xn, w_in, spatial_w, spatial_b, g_b, col0, gw, w_out, tm=1024, tn=512,
                rcs=(256, 256, 256, 128, 128), cc=256):
    t, d = xn.shape
    nj, ni = gw // tn, t // tm
    assert sum(rcs) == tm and all(rc % CHUNK == 0 for rc in rcs)
    assert tn % cc == 0 and cc % HEAD_DIM == 0
    off_u = col0 // tn
    sb = jnp.broadcast_to(spatial_b[:, :, None], spatial_b.shape + (HEAD_DIM,))
    kern = functools.partial(_gmlp_kernel, rcs, cc)
    next_col = lambda p: jnp.minimum(p, nj - 1)
    wo_spec = _pass_slab_spec(w_out.shape[0], w_out.shape[1], nj, ni)
    return pl.pallas_call(
        kern,
        grid=(nj + 1, ni),
        in_specs=[pl.BlockSpec((tm, d), lambda p, i: (_pass_row_tile(p, i), 0)),
                  pl.BlockSpec((d // ni, tn), lambda p, i: (i, off_u + next_col(p))),
                  pl.BlockSpec((d // ni, tn), lambda p, i: (i, off_u + nj + next_col(p))),
                  _WHOLE_VMEM, _WHOLE_VMEM, _WHOLE_VMEM,
                  wo_spec],
        out_specs=[pl.BlockSpec((tm, tn),
                                lambda p, i: (_pass_row_tile(p, i), _pass_col_tile(p))),
                   _ssq_spec(tm, _pass_row_tile, _pass_col_tile),
                   wo_spec],
        out_shape=[jax.ShapeDtypeStruct((t, gw), jnp.bfloat16),
                   jax.ShapeDtypeStruct((nj, t, LANES), jnp.float32),
                   jax.ShapeDtypeStruct(w_out.shape, jnp.bfloat16)],
        scratch_shapes=[pltpu.VMEM((2, d, tn), jnp.bfloat16),
                        pltpu.VMEM((2, d, tn), jnp.bfloat16)],
        compiler_params=_params(_ARB2),
        name="gmlp_mixer",
    )(xn, w_in, w_in, spatial_w, sb, g_b.reshape(nj, tn), w_out)


def _out_proj_kernel(ccs, ya_ref, yb_ref, wa_ref, wb_ref, sa_ref, sb_ref, x_ref, g_ref,
                     h_ref, xg_ref, ssq_ref):
    ya, yb = ya_ref[...], yb_ref[...]
    inv_a = _inv_rms(sa_ref, ya.shape[1])
    inv_b = _inv_rms(sb_ref, yb.shape[1])
    g = g_ref[pl.ds(pl.program_id(0), 1), :]
    sq = None
    for c0, cc in _chunks(ccs):
        cs = slice(c0, c0 + cc)
        h = x_ref[:, cs] + _bdot(ya, wa_ref[:, cs]) * inv_a + _bdot(yb, wb_ref[:, cs]) * inv_b
        h_ref[:, cs] = h
        xg_ref[:, cs] = (h * g[:, cs]).astype(xg_ref.dtype)
        part = _lane_group_sum(h * h)
        sq = part if sq is None else sq + part
    ssq_ref[0] = sq


def _out_proj(ya, yb, w_out, ssq_a, ssq_b, x, g_mlp, tm=512, tn=2048,
              ccs=(512, 512, 512, 256, 256)):
    t, cw = ya.shape
    gw = yb.shape[1]
    d = w_out.shape[1]
    assert cw == gw and sum(ccs) == tn
    kern = functools.partial(_out_proj_kernel, ccs)
    ssq_in = lambda a: pl.BlockSpec((a.shape[0], tm, LANES), lambda j, i: (0, i, 0))
    tile_spec = pl.BlockSpec((tm, tn), lambda j, i: (i, j))
    return pl.pallas_call(
        kern,
        grid=(d // tn, t // tm),
        in_specs=[pl.BlockSpec((tm, cw), lambda j, i: (i, 0)),
                  pl.BlockSpec((tm, gw), lambda j, i: (i, 0)),
                  pl.BlockSpec((cw, tn), lambda j, i: (0, j), pipeline_mode=pl.Buffered(1)),
                  pl.BlockSpec((gw, tn), lambda j, i: (1, j), pipeline_mode=pl.Buffered(1)),
                  ssq_in(ssq_a), ssq_in(ssq_b), tile_spec, _WHOLE_VMEM],
        out_specs=[tile_spec, tile_spec,
                   _ssq_spec(tm, lambda j, i: i, lambda j: j)],
        out_shape=[jax.ShapeDtypeStruct((t, d), jnp.float32),
                   jax.ShapeDtypeStruct((t, d), jnp.bfloat16),
                   jax.ShapeDtypeStruct((d // tn, t, LANES), jnp.float32)],
        compiler_params=_params(_ARB2),
        name="out_proj",
    )(ya, yb, w_out, w_out, ssq_a, ssq_b, x, g_mlp.reshape(d // tn, tn))


def _mlp_up_kernel(rc, x_ref, wc_ref, ssq_ref, wd_ref, r_ref, wdb_ref, wbf_a, wbf_b):
    kc = wc_ref.shape[0]
    chunk_rows = pl.ds(pl.multiple_of(pl.program_id(1) * kc, kc), kc)
    wdb_ref[...] = wd_ref[...].astype(wdb_ref.dtype)

    def step(cur_ref, nxt_ref):
        nxt_ref[chunk_rows, :] = wc_ref[...].astype(nxt_ref.dtype)
        if cur_ref is None:
            return
        inv = _inv_rms(ssq_ref, x_ref.shape[1])
        for r in range(r_ref.shape[0] // rc):
            rs = slice(r * rc, (r + 1) * rc)
            z = jnp.maximum(_bdot(x_ref[rs, :], cur_ref[...]) * inv[rs, :], 0.0)
            r_ref[rs, :] = (z * z).astype(r_ref.dtype)

    _run_streamed_pass(step, wbf_a, wbf_b)


def _mlp_up(xg, w_up, ssq, w_down, tm=1024, tn=1024, rc=512):
    t, d = xg.shape
    f = w_up.shape[1]
    nj, ni = f // tn, t // tm
    kern = functools.partial(_mlp_up_kernel, rc)
    slab_spec = _pass_slab_spec(f, d, nj, ni)
    return pl.pallas_call(
        kern,
        grid=(nj + 1, ni),
        in_specs=[pl.BlockSpec((tm, d), lambda p, i: (_pass_row_tile(p, i), 0)),
                  pl.BlockSpec((d // ni, tn), lambda p, i: (i, jnp.minimum(p, nj - 1))),
                  pl.BlockSpec((ssq.shape[0], tm, LANES),
                               lambda p, i: (0, _pass_row_tile(p, i), 0)),
                  slab_spec],
        out_specs=[pl.BlockSpec((tm, tn),
                                lambda p, i: (_pass_row_tile(p, i), _pass_col_tile(p))),
                   slab_spec],
        out_shape=[jax.ShapeDtypeStruct((t, f), jnp.bfloat16),
                   jax.ShapeDtypeStruct(w_down.shape, jnp.bfloat16)],
        scratch_shapes=[pltpu.VMEM((d, tn), jnp.bfloat16),
                        pltpu.VMEM((d, tn), jnp.bfloat16)],
        compiler_params=_params(_ARB2),
        name="mlp_up",
    )(xg, w_up, ssq, w_down)


def _mlp_down_kernel(r_ref, w_ref, h_ref, o_ref):
    k = pl.program_id(2)
    last = pl.num_programs(2) - 1

    @pl.when(k == 0)
    def _():
        o_ref[...] = _bdot(r_ref[...], w_ref[...])

    @pl.when((k > 0) & (k < last))
    def _():
        o_ref[...] += _bdot(r_ref[...], w_ref[...])

    @pl.when(k == last)
    def _():
        o_ref[...] = h_ref[...] + (o_ref[...] + _bdot(r_ref[...], w_ref[...]))


def _mlp_down(r, w_down, h, tm=1024, tn=1024, tk=4096):
    t, f = r.shape
    d = w_down.shape[1]
    assert f // tk >= 2
    return pl.pallas_call(
        _mlp_down_kernel,
        grid=(d // tn, t // tm, f // tk),
        in_specs=[pl.BlockSpec((tm, tk), lambda j, i, k: (i, k)),
                  pl.BlockSpec((tk, tn), lambda j, i, k: (k, j)),
                  pl.BlockSpec((tm, tn), lambda j, i, k: (i, j))],
        out_specs=pl.BlockSpec((tm, tn), lambda j, i, k: (i, j)),
        out_shape=jax.ShapeDtypeStruct((t, d), jnp.float32),
        compiler_params=_params(("arbitrary", "arbitrary", "arbitrary")),
        name="mlp_down",
    )(r, w_down, h)


def kernel(x, mix_norm_g, w_in, conv_w, spatial_w, spatial_b, conv_out_norm_g,
           gmlp_out_norm_g, w_out, mlp_norm_g, w_up, w_down, final_norm_g):
    bsz, seq, d = x.shape
    depth = w_in.shape[0]
    cw = conv_w.shape[2]
    gw = spatial_w.shape[1] * HEAD_DIM
    assert seq % CHUNK == 0 and w_in.shape[2] == 3 * cw + 2 * gw
    h = x.reshape(bsz * seq, d)
    for l in range(depth):
        xn = _rmsnorm(h, mix_norm_g[l], jnp.bfloat16)
        ya, ssq_a = _conv_mixer(xn, w_in[l], conv_w[l], conv_out_norm_g[l], seq, cw)
        yb, ssq_b, w_out_bf16 = _gmlp_mixer(xn, w_in[l], spatial_w[l], spatial_b[l],
                                            gmlp_out_norm_g[l], 3 * cw, gw, w_out[l])
        h, xg, ssq_h = _out_proj(ya, yb, w_out_bf16, ssq_a, ssq_b, h, mlp_norm_g[l])
        r, w_down_bf16 = _mlp_up(xg, w_up[l], ssq_h, w_down[l])
        h = _mlp_down(r, w_down_bf16, h)
    out = _rmsnorm(h, final_norm_g, x.dtype)
    return out.reshape(bsz, seq, d)
```

```python
import functools

import jax
import jax.numpy as jnp
from jax import lax
from jax.experimental import pallas as pl
from jax.experimental.pallas import tpu as pltpu

EPS = 1e-5
HEAD_DIM = 128
CHUNK = 128
CONV_K = 3
LANES = 128
HALO = 8

_VMEM_LIMIT = 60000 * 1024

_ARB2 = ("arbitrary", "arbitrary")


def _params(sem):
    return pltpu.CompilerParams(dimension_semantics=sem, vmem_limit_bytes=_VMEM_LIMIT)


def _bdot(a, b):
    return jnp.dot(a.astype(jnp.bfloat16), b.astype(jnp.bfloat16),
                   preferred_element_type=jnp.float32)


def _lane_group_sum(sq):
    acc = sq[:, 0:LANES]
    for c in range(1, sq.shape[1] // LANES):
        acc = acc + sq[:, c * LANES:(c + 1) * LANES]
    return acc


def _ssq_spec(tm, row_tile, col_tile):
    return pl.BlockSpec((1, tm, LANES), lambda a, i: (col_tile(a), row_tile(a, i), 0))


def _inv_rms(ssq_ref, width):
    total = jnp.sum(jnp.sum(ssq_ref[...], axis=0), axis=-1, keepdims=True)
    return lax.rsqrt(total / width + EPS)


_WHOLE_VMEM = pl.BlockSpec(memory_space=pltpu.VMEM)


def _chunks(sizes):
    out, start = [], 0
    for size in sizes:
        out.append((start, size))
        start += size
    return out


def _pass_row_tile(p, i):
    return jnp.where(p == 0, 0, i)


def _pass_col_tile(p):
    return jnp.maximum(p - 1, 0)


def _pass_slab_spec(rows, cols, nj, ni):
    return pl.BlockSpec((rows // (nj * ni), cols),
                        lambda p, i: (_pass_col_tile(p) * ni + _pass_row_tile(p, i), 0))


def _run_streamed_pass(step, wbf_a, wbf_b):
    p = pl.program_id(0)
    pl.when(p == 0)(lambda: step(None, wbf_a))
    pl.when((p > 0) & (p % 2 == 1))(lambda: step(wbf_a, wbf_b))
    pl.when((p > 0) & (p % 2 == 0))(lambda: step(wbf_b, wbf_a))


def _rmsnorm_kernel(x_ref, g_ref, o_ref):
    x = x_ref[...]
    ms = jnp.mean(x * x, axis=-1, keepdims=True)
    o_ref[...] = (x * lax.rsqrt(ms + EPS) * g_ref[...]).astype(o_ref.dtype)


def _rmsnorm(x, g, out_dtype, tr=512):
    t, d = x.shape
    assert t % tr == 0
    return pl.pallas_call(
        _rmsnorm_kernel,
        grid=(t // tr,),
        in_specs=[pl.BlockSpec((tr, d), lambda i: (i, 0)),
                  pl.BlockSpec((1, d), lambda i: (0, 0))],
        out_specs=pl.BlockSpec((tr, d), lambda i: (i, 0)),
        out_shape=jax.ShapeDtypeStruct((t, d), out_dtype),
        compiler_params=_params(("arbitrary",)),
        name="rmsnorm_cast",
    )(x, g.reshape(1, d))


def _conv_kernel(tiles_per_seq, rcs, x_ref, wb_ref, wc_ref, wh_ref, cw_ref, g_ref,
                 y_ref, ssq_ref, wbf_ref, ext_ref):
    j, i = pl.program_id(0), pl.program_id(1)
    tm, tn = y_ref.shape
    taps = [cw_ref[k, pl.ds(j, 1), :] for k in range(CONV_K)]
    g = g_ref[pl.ds(j, 1), :]

    @pl.when(i == 0)
    def _():
        for n, w_ref in enumerate((wb_ref, wc_ref, wh_ref)):
            wbf_ref[n] = w_ref[...].astype(wbf_ref.dtype)

    @pl.when(i % tiles_per_seq == 0)
    def _():
        ext_ref[0:HALO, :] = jnp.zeros((HALO, tn), jnp.float32)

    for r0, rc in _chunks(rcs):
        rs = slice(r0, r0 + rc)
        xb = x_ref[rs, :]
        bg = _bdot(xb, wbf_ref[0])
        ch = _bdot(xb, wbf_ref[1]) * _bdot(xb, wbf_ref[2])
        base = HALO + r0
        ext_ref[base:base + rc, :] = ch
        conv = taps[CONV_K - 1] * ch
        for k in range(CONV_K - 1):
            shift = CONV_K - 1 - k
            conv = conv + taps[k] * ext_ref[base - shift:base - shift + rc, :]
        ya = bg * conv
        y_ref[rs, :] = (ya * g).astype(y_ref.dtype)
        ssq_ref[0, rs, :] = _lane_group_sum(ya * ya)
    ext_ref[0:HALO, :] = ext_ref[tm:tm + HALO, :]


def _conv_mixer(xn, w_in, conv_w, g_a, seq, cw, tm=1024, tn=256,
                rcs=(256, 256, 256, 128, 128)):
    t, d = xn.shape
    nj = cw // tn
    assert sum(rcs) == tm and seq % tm == 0 and t % tm == 0 and cw % tn == 0
    kern = functools.partial(_conv_kernel, seq // tm, rcs)
    wspec = lambda off: pl.BlockSpec((d, tn), lambda j, i, off=off: (0, j + off))
    return pl.pallas_call(
        kern,
        grid=(nj, t // tm),
        in_specs=[pl.BlockSpec((tm, d), lambda j, i: (i, 0)),
                  wspec(0), wspec(nj), wspec(2 * nj), _WHOLE_VMEM, _WHOLE_VMEM],
        out_specs=[pl.BlockSpec((tm, tn), lambda j, i: (i, j)),
                   _ssq_spec(tm, lambda j, i: i, lambda j: j)],
        out_shape=[jax.ShapeDtypeStruct((t, cw), jnp.bfloat16),
                   jax.ShapeDtypeStruct((nj, t, LANES), jnp.float32)],
        scratch_shapes=[pltpu.VMEM((3, d, tn), jnp.bfloat16),
                        pltpu.VMEM((HALO + tm, tn), jnp.float32)],
        compiler_params=_params(_ARB2),
        name="conv_mixer",
    )(xn, w_in, w_in, w_in, conv_w.reshape(CONV_K, nj, tn), g_a.reshape(nj, tn))


def _gmlp_kernel(rcs, cc, x_ref, wuc_ref, wvc_ref, sw_ref, sb_ref, g_ref, wo_ref,
                 y_ref, ssq_ref, wob_ref, wbf_a, wbf_b):
    tm, tn = y_ref.shape
    kc = wuc_ref.shape[0]
    chunk_rows = pl.ds(pl.multiple_of(pl.program_id(1) * kc, kc), kc)
    wob_ref[...] = wo_ref[...].astype(wob_ref.dtype)

    def step(cur_ref, nxt_ref):
        nxt_ref[0, chunk_rows, :] = wuc_ref[...].astype(nxt_ref.dtype)
        nxt_ref[1, chunk_rows, :] = wvc_ref[...].astype(nxt_ref.dtype)
        if cur_ref is None:
            return
        col_tile = pl.program_id(0) - 1
        row = lax.broadcasted_iota(jnp.int32, (CHUNK, CHUNK), 0)
        col = lax.broadcasted_iota(jnp.int32, (CHUNK, CHUNK), 1)
        causal = col <= row
        heads = tn // HEAD_DIM
        head0 = col_tile * heads
        ws = [jnp.where(causal, sw_ref[head0 + h], 0.0).astype(jnp.bfloat16)
              for h in range(heads)]
        bias = [sb_ref[head0 + h] for h in range(heads)]
        g = g_ref[pl.ds(col_tile, 1), :]
        units = [(r0, rc, c0) for r0, rc in _chunks(rcs) for c0 in range(0, tn, cc)]
        sub_heads = cc // HEAD_DIM

        def uv_dots(r0, rc, c0):
            xb = x_ref[r0:r0 + rc, :]
            return (_bdot(xb, cur_ref[0, :, c0:c0 + cc]),
                    _bdot(xb, cur_ref[1, :, c0:c0 + cc]))

        uv = uv_dots(*units[0])
        sq = None
        for n, (r0, rc, c0) in enumerate(units):
            u, v = uv
            if n + 1 < len(units):
                uv = uv_dots(*units[n + 1])
            gu = jax.nn.gelu(u)
            gv = jax.nn.gelu(v).astype(jnp.bfloat16)
            n_sub = rc // CHUNK
            h0 = c0 // HEAD_DIM
            gate = []
            for h in range(sub_heads):
                hs = slice(h * HEAD_DIM, (h + 1) * HEAD_DIM)
                vcat = jnp.concatenate(
                    [gv[c * CHUNK:(c + 1) * CHUNK, hs] for c in range(n_sub)], axis=1)
                gate.append(_bdot(ws[h0 + h], vcat))
            s = jnp.concatenate(
                [jnp.concatenate([gate[h][:, c * CHUNK:(c + 1) * CHUNK] + bias[h0 + h]
                                  for h in range(sub_heads)], axis=1)
                 for c in range(n_sub)], axis=0)
            yb = gu * s
            y_ref[r0:r0 + rc, c0:c0 + cc] = (yb * g[:, c0:c0 + cc]).astype(y_ref.dtype)
            part = _lane_group_sum(yb * yb)
            sq = part if c0 == 0 else sq + part
            if c0 + cc == tn:
                ssq_ref[0, r0:r0 + rc, :] = sq

    _run_streamed_pass(step, wbf_a, wbf_b)


def _gmlp_mixer(xn, w_in, spatial_w, spatial_b, g_b, col0, gw, w_out, tm=1024, tn=512,
                rcs=(256, 256, 256, 128, 128), cc=256):
    t, d = xn.shape
    nj, ni = gw // tn, t // tm
    assert sum(rcs) == tm and all(rc % CHUNK == 0 for rc in rcs)
    assert tn % cc == 0 and cc % HEAD_DIM == 0 and gw % tn == 0 and t % tm == 0
    assert d % ni == 0 and w_out.shape[0] % (nj * ni) == 0
    off_u = col0 // tn
    sb = jnp.broadcast_to(spatial_b[:, :, None], spatial_b.shape + (HEAD_DIM,))
    kern = functools.partial(_gmlp_kernel, rcs, cc)
    next_col = lambda p: jnp.minimum(p, nj - 1)
    wo_spec = _pass_slab_spec(w_out.shape[0], w_out.shape[1], nj, ni)
    return pl.pallas_call(
        kern,
        grid=(nj + 1, ni),
        in_specs=[pl.BlockSpec((tm, d), lambda p, i: (_pass_row_tile(p, i), 0)),
                  pl.BlockSpec((d // ni, tn), lambda p, i: (i, off_u + next_col(p))),
                  pl.BlockSpec((d // ni, tn), lambda p, i: (i, off_u + nj + next_col(p))),
                  _WHOLE_VMEM, _WHOLE_VMEM, _WHOLE_VMEM,
                  wo_spec],
        out_specs=[pl.BlockSpec((tm, tn),
                                lambda p, i: (_pass_row_tile(p, i), _pass_col_tile(p))),
                   _ssq_spec(tm, _pass_row_tile, _pass_col_tile),
                   wo_spec],
        out_shape=[jax.ShapeDtypeStruct((t, gw), jnp.bfloat16),
                   jax.ShapeDtypeStruct((nj, t, LANES), jnp.float32),
                   jax.ShapeDtypeStruct(w_out.shape, jnp.bfloat16)],
        scratch_shapes=[pltpu.VMEM((2, d, tn), jnp.bfloat16),
                        pltpu.VMEM((2, d, tn), jnp.bfloat16)],
        compiler_params=_params(_ARB2),
        name="gmlp_mixer",
    )(xn, w_in, w_in, spatial_w, sb, g_b.reshape(nj, tn), w_out)


def _out_proj_kernel(ccs, ya_ref, yb_ref, wa_ref, wb_ref, sa_ref, sb_ref, x_ref, g_ref,
                     h_ref, xg_ref, ssq_ref):
    ya, yb = ya_ref[...], yb_ref[...]
    inv_a = _inv_rms(sa_ref, ya.shape[1])
    inv_b = _inv_rms(sb_ref, yb.shape[1])
    g = g_ref[pl.ds(pl.program_id(0), 1), :]
    sq = None
    for c0, cc in _chunks(ccs):
        cs = slice(c0, c0 + cc)
        h = x_ref[:, cs] + _bdot(ya, wa_ref[:, cs]) * inv_a + _bdot(yb, wb_ref[:, cs]) * inv_b
        h_ref[:, cs] = h
        xg_ref[:, cs] = (h * g[:, cs]).astype(xg_ref.dtype)
        part = _lane_group_sum(h * h)
        sq = part if sq is None else sq + part
    ssq_ref[0] = sq


def _out_proj(ya, yb, w_out, ssq_a, ssq_b, x, g_mlp, tm=512, tn=2048,
              ccs=(512, 512, 512, 256, 256)):
    t, cw = ya.shape
    gw = yb.shape[1]
    d = w_out.shape[1]
    assert cw == gw and sum(ccs) == tn and d % tn == 0 and t % tm == 0
    kern = functools.partial(_out_proj_kernel, ccs)
    ssq_in = lambda a: pl.BlockSpec((a.shape[0], tm, LANES), lambda j, i: (0, i, 0))
    tile_spec = pl.BlockSpec((tm, tn), lambda j, i: (i, j))
    return pl.pallas_call(
        kern,
        grid=(d // tn, t // tm),
        in_specs=[pl.BlockSpec((tm, cw), lambda j, i: (i, 0)),
                  pl.BlockSpec((tm, gw), lambda j, i: (i, 0)),
                  pl.BlockSpec((cw, tn), lambda j, i: (0, j), pipeline_mode=pl.Buffered(1)),
                  pl.BlockSpec((gw, tn), lambda j, i: (1, j), pipeline_mode=pl.Buffered(1)),
                  ssq_in(ssq_a), ssq_in(ssq_b), tile_spec, _WHOLE_VMEM],
        out_specs=[tile_spec, tile_spec,
                   _ssq_spec(tm, lambda j, i: i, lambda j: j)],
        out_shape=[jax.ShapeDtypeStruct((t, d), jnp.float32),
                   jax.ShapeDtypeStruct((t, d), jnp.bfloat16),
                   jax.ShapeDtypeStruct((d // tn, t, LANES), jnp.float32)],
        compiler_params=_params(_ARB2),
        name="out_proj",
    )(ya, yb, w_out, w_out, ssq_a, ssq_b, x, g_mlp.reshape(d // tn, tn))


def _mlp_up_kernel(rc, x_ref, wc_ref, ssq_ref, wd_ref, r_ref, wdb_ref, wbf_a, wbf_b):
    kc = wc_ref.shape[0]
    chunk_rows = pl.ds(pl.multiple_of(pl.program_id(1) * kc, kc), kc)
    wdb_ref[...] = wd_ref[...].astype(wdb_ref.dtype)

    def step(cur_ref, nxt_ref):
        nxt_ref[chunk_rows, :] = wc_ref[...].astype(nxt_ref.dtype)
        if cur_ref is None:
            return
        inv = _inv_rms(ssq_ref, x_ref.shape[1])
        for r in range(r_ref.shape[0] // rc):
            rs = slice(r * rc, (r + 1) * rc)
            z = jnp.maximum(_bdot(x_ref[rs, :], cur_ref[...]) * inv[rs, :], 0.0)
            r_ref[rs, :] = (z * z).astype(r_ref.dtype)

    _run_streamed_pass(step, wbf_a, wbf_b)


def _mlp_up(xg, w_up, ssq, w_down, tm=1024, tn=1024, rc=512):
    t, d = xg.shape
    f = w_up.shape[1]
    nj, ni = f // tn, t // tm
    assert f % tn == 0 and t % tm == 0 and tm % rc == 0 and d % ni == 0 and f % (nj * ni) == 0
    kern = functools.partial(_mlp_up_kernel, rc)
    slab_spec = _pass_slab_spec(f, d, nj, ni)
    return pl.pallas_call(
        kern,
        grid=(nj + 1, ni),
        in_specs=[pl.BlockSpec((tm, d), lambda p, i: (_pass_row_tile(p, i), 0)),
                  pl.BlockSpec((d // ni, tn), lambda p, i: (i, jnp.minimum(p, nj - 1))),
                  pl.BlockSpec((ssq.shape[0], tm, LANES),
                               lambda p, i: (0, _pass_row_tile(p, i), 0)),
                  slab_spec],
        out_specs=[pl.BlockSpec((tm, tn),
                                lambda p, i: (_pass_row_tile(p, i), _pass_col_tile(p))),
                   slab_spec],
        out_shape=[jax.ShapeDtypeStruct((t, f), jnp.bfloat16),
                   jax.ShapeDtypeStruct(w_down.shape, jnp.bfloat16)],
        scratch_shapes=[pltpu.VMEM((d, tn), jnp.bfloat16),
                        pltpu.VMEM((d, tn), jnp.bfloat16)],
        compiler_params=_params(_ARB2),
        name="mlp_up",
    )(xg, w_up, ssq, w_down)


def _mlp_down_kernel(r_ref, w_ref, h_ref, o_ref):
    k = pl.program_id(2)
    last = pl.num_programs(2) - 1

    @pl.when(k == 0)
    def _():
        o_ref[...] = _bdot(r_ref[...], w_ref[...])

    @pl.when((k > 0) & (k < last))
    def _():
        o_ref[...] += _bdot(r_ref[...], w_ref[...])

    @pl.when(k == last)
    def _():
        o_ref[...] = h_ref[...] + (o_ref[...] + _bdot(r_ref[...], w_ref[...]))


def _mlp_down(r, w_down, h, tm=1024, tn=1024, tk=4096):
    t, f = r.shape
    d = w_down.shape[1]
    assert d % tn == 0 and t % tm == 0 and f % tk == 0
    assert f // tk >= 2
    return pl.pallas_call(
        _mlp_down_kernel,
        grid=(d // tn, t // tm, f // tk),
        in_specs=[pl.BlockSpec((tm, tk), lambda j, i, k: (i, k)),
                  pl.BlockSpec((tk, tn), lambda j, i, k: (k, j)),
                  pl.BlockSpec((tm, tn), lambda j, i, k: (i, j))],
        out_specs=pl.BlockSpec((tm, tn), lambda j, i, k: (i, j)),
        out_shape=jax.ShapeDtypeStruct((t, d), jnp.float32),
        compiler_params=_params(("arbitrary", "arbitrary", "arbitrary")),
        name="mlp_down",
    )(r, w_down, h)


def kernel(x, mix_norm_g, w_in, conv_w, spatial_w, spatial_b, conv_out_norm_g,
           gmlp_out_norm_g, w_out, mlp_norm_g, w_up, w_down, final_norm_g):
    bsz, seq, d = x.shape
    depth = w_in.shape[0]
    cw = conv_w.shape[2]
    gw = spatial_w.shape[1] * HEAD_DIM
    assert seq % CHUNK == 0 and w_in.shape[2] == 3 * cw + 2 * gw
    h = x.reshape(bsz * seq, d)
    for l in range(depth):
        xn = _rmsnorm(h, mix_norm_g[l], jnp.bfloat16)
        ya, ssq_a = _conv_mixer(xn, w_in[l], conv_w[l], conv_out_norm_g[l], seq, cw)
        yb, ssq_b, w_out_bf16 = _gmlp_mixer(xn, w_in[l], spatial_w[l], spatial_b[l],
                                            gmlp_out_norm_g[l], 3 * cw, gw, w_out[l])
        h, xg, ssq_h = _out_proj(ya, yb, w_out_bf16, ssq_a, ssq_b, h, mlp_norm_g[l])
        r, w_down_bf16 = _mlp_up(xg, w_up[l], ssq_h, w_down[l])
        h = _mlp_down(r, w_down_bf16, h)
    out = _rmsnorm(h, final_norm_g, x.dtype)
    return out.reshape(bsz, seq, d)
```

```python
import functools

import jax
import jax.numpy as jnp
from jax import lax
from jax.experimental import pallas as pl
from jax.experimental.pallas import tpu as pltpu

EPS = 1e-5
HEAD_DIM = 128
CHUNK = 128
CONV_K = 3
LANES = 128
HALO = 16

_VMEM_LIMIT = 60000 * 1024

_ARB2 = ("arbitrary", "arbitrary")


def _params(sem):
    return pltpu.CompilerParams(dimension_semantics=sem, vmem_limit_bytes=_VMEM_LIMIT)


def _bdot(a, b):
    return jnp.dot(a.astype(jnp.bfloat16), b.astype(jnp.bfloat16),
                   preferred_element_type=jnp.float32)


def _lane_group_sum(sq):
    acc = sq[:, 0:LANES]
    for c in range(1, sq.shape[1] // LANES):
        acc = acc + sq[:, c * LANES:(c + 1) * LANES]
    return acc


def _ssq_spec(tm, row_tile, col_tile):
    return pl.BlockSpec((1, tm, LANES), lambda a, i: (col_tile(a), row_tile(a, i), 0))


def _inv_rms(ssq_ref, width):
    total = jnp.sum(jnp.sum(ssq_ref[...], axis=0), axis=-1, keepdims=True)
    return lax.rsqrt(total / width + EPS)


_WHOLE_VMEM = pl.BlockSpec(memory_space=pltpu.VMEM)


def _chunks(sizes):
    out, start = [], 0
    for size in sizes:
        out.append((start, size))
        start += size
    return out


def _pass_row_tile(p, i):
    return jnp.where(p == 0, 0, i)


def _pass_col_tile(p):
    return jnp.maximum(p - 1, 0)


def _pass_slab_spec(rows, cols, nj, ni):
    return pl.BlockSpec((rows // (nj * ni), cols),
                        lambda p, i: (_pass_col_tile(p) * ni + _pass_row_tile(p, i), 0))


def _run_streamed_pass(step, wbf_a, wbf_b):
    p = pl.program_id(0)
    pl.when(p == 0)(lambda: step(None, wbf_a))
    pl.when((p > 0) & (p % 2 == 1))(lambda: step(wbf_a, wbf_b))
    pl.when((p > 0) & (p % 2 == 0))(lambda: step(wbf_b, wbf_a))


def _rmsnorm_kernel(x_ref, g_ref, o_ref):
    x = x_ref[...]
    ms = jnp.mean(x * x, axis=-1, keepdims=True)
    o_ref[...] = (x * lax.rsqrt(ms + EPS) * g_ref[...]).astype(o_ref.dtype)


def _rmsnorm(x, g, out_dtype, tr=512):
    t, d = x.shape
    assert t % tr == 0
    return pl.pallas_call(
        _rmsnorm_kernel,
        grid=(t // tr,),
        in_specs=[pl.BlockSpec((tr, d), lambda i: (i, 0)),
                  pl.BlockSpec((1, d), lambda i: (0, 0))],
        out_specs=pl.BlockSpec((tr, d), lambda i: (i, 0)),
        out_shape=jax.ShapeDtypeStruct((t, d), out_dtype),
        compiler_params=_params(("arbitrary",)),
        name="rmsnorm_cast",
    )(x, g.reshape(1, d))


def _conv_kernel(tiles_per_seq, rcs, x_ref, wb_ref, wc_ref, wh_ref, cw_ref, g_ref,
                 y_ref, ssq_ref, wbf_ref, ext_ref):
    j, i = pl.program_id(0), pl.program_id(1)
    tm, tn = y_ref.shape
    taps = [cw_ref[k, j] for k in range(CONV_K)]
    g = g_ref[j]

    @pl.when(i == 0)
    def _():
        for n, w_ref in enumerate((wb_ref, wc_ref, wh_ref)):
            wbf_ref[n] = w_ref[...].astype(wbf_ref.dtype)

    @pl.when(i % tiles_per_seq == 0)
    def _():
        ext_ref[0:HALO, :] = jnp.zeros((HALO, tn), jnp.float32)

    for r0, rc in _chunks(rcs):
        rs = slice(r0, r0 + rc)
        xb = x_ref[rs, :]
        bg = _bdot(xb, wbf_ref[0])
        ch = _bdot(xb, wbf_ref[1]) * _bdot(xb, wbf_ref[2])
        base = HALO + r0
        ext_ref[base:base + rc, :] = ch
        conv = taps[CONV_K - 1] * ch
        for k in range(CONV_K - 1):
            shift = CONV_K - 1 - k
            conv = conv + taps[k] * ext_ref[base - shift:base - shift + rc, :]
        ya = bg * conv
        y_ref[rs, :] = (ya * g).astype(y_ref.dtype)
        ssq_ref[0, rs, :] = _lane_group_sum(ya * ya)
    ext_ref[0:HALO, :] = ext_ref[tm:tm + HALO, :]


def _conv_mixer(xn, w_in, conv_w, g_a, seq, cw, tm=1024, tn=256,
                rcs=(256, 256, 256, 128, 128)):
    t, d = xn.shape
    nj = cw // tn
    assert sum(rcs) == tm and seq % tm == 0 and t % tm == 0 and cw % tn == 0
    kern = functools.partial(_conv_kernel, seq // tm, rcs)
    wspec = lambda off: pl.BlockSpec((d, tn), lambda j, i, off=off: (0, j + off))
    return pl.pallas_call(
        kern,
        grid=(nj, t // tm),
        in_specs=[pl.BlockSpec((tm, d), lambda j, i: (i, 0)),
                  wspec(0), wspec(nj), wspec(2 * nj), _WHOLE_VMEM, _WHOLE_VMEM],
        out_specs=[pl.BlockSpec((tm, tn), lambda j, i: (i, j)),
                   _ssq_spec(tm, lambda j, i: i, lambda j: j)],
        out_shape=[jax.ShapeDtypeStruct((t, cw), jnp.bfloat16),
                   jax.ShapeDtypeStruct((nj, t, LANES), jnp.float32)],
        scratch_shapes=[pltpu.VMEM((3, d, tn), jnp.bfloat16),
                        pltpu.VMEM((HALO + tm, tn), jnp.float32)],
        compiler_params=_params(_ARB2),
        name="conv_mixer",
    )(xn, w_in, w_in, w_in, conv_w.reshape(CONV_K, nj, 1, tn), g_a.reshape(nj, 1, tn))


def _gmlp_kernel(rcs, cc, x_ref, wuc_ref, wvc_ref, sw_ref, sb_ref, g_ref, wo_ref,
                 y_ref, ssq_ref, wob_ref, wbf_a, wbf_b):
    tm, tn = y_ref.shape
    kc = wuc_ref.shape[0]
    chunk_rows = pl.ds(pl.multiple_of(pl.program_id(1) * kc, kc), kc)
    wob_ref[...] = wo_ref[...].astype(wob_ref.dtype)

    def step(cur_ref, nxt_ref):
        nxt_ref[0, chunk_rows, :] = wuc_ref[...].astype(nxt_ref.dtype)
        nxt_ref[1, chunk_rows, :] = wvc_ref[...].astype(nxt_ref.dtype)
        if cur_ref is None:
            return
        col_tile = pl.program_id(0) - 1
        row = lax.broadcasted_iota(jnp.int32, (CHUNK, CHUNK), 0)
        col = lax.broadcasted_iota(jnp.int32, (CHUNK, CHUNK), 1)
        causal = col <= row
        heads = tn // HEAD_DIM
        head0 = col_tile * heads
        ws = [jnp.where(causal, sw_ref[head0 + h], 0.0).astype(jnp.bfloat16)
              for h in range(heads)]
        bias = [jnp.broadcast_to(sb_ref[pl.ds(head0 + h, 1), :], (HEAD_DIM, CHUNK)).T
                for h in range(heads)]
        g = g_ref[col_tile]
        units = [(r0, rc, c0) for r0, rc in _chunks(rcs) for c0 in range(0, tn, cc)]
        sub_heads = cc // HEAD_DIM

        def uv_dots(r0, rc, c0):
            xb = x_ref[r0:r0 + rc, :]
            return (_bdot(xb, cur_ref[0, :, c0:c0 + cc]),
                    _bdot(xb, cur_ref[1, :, c0:c0 + cc]))

        uv = uv_dots(*units[0])
        sq = None
        for n, (r0, rc, c0) in enumerate(units):
            u, v = uv
            if n + 1 < len(units):
                uv = uv_dots(*units[n + 1])
            gu = jax.nn.gelu(u)
            gv = jax.nn.gelu(v).astype(jnp.bfloat16)
            n_sub = rc // CHUNK
            h0 = c0 // HEAD_DIM
            gate = []
            for h in range(sub_heads):
                hs = slice(h * HEAD_DIM, (h + 1) * HEAD_DIM)
                vcat = jnp.concatenate(
                    [gv[c * CHUNK:(c + 1) * CHUNK, hs] for c in range(n_sub)], axis=1)
                gate.append(_bdot(ws[h0 + h], vcat))
            s = jnp.concatenate(
                [jnp.concatenate([gate[h][:, c * CHUNK:(c + 1) * CHUNK] + bias[h0 + h]
                                  for h in range(sub_heads)], axis=1)
                 for c in range(n_sub)], axis=0)
            yb = gu * s
            y_ref[r0:r0 + rc, c0:c0 + cc] = (yb * g[:, c0:c0 + cc]).astype(y_ref.dtype)
            part = _lane_group_sum(yb * yb)
            sq = part if c0 == 0 else sq + part
            if c0 + cc == tn:
                ssq_ref[0, r0:r0 + rc, :] = sq

    _run_streamed_pass(step, wbf_a, wbf_b)


def _gmlp_mixer(xn, w_in, spatial_w, spatial_b, g_b, col0, gw, w_out, tm=1024, tn=512,
                rcs=(256, 256, 256, 128, 128), cc=256):
    t, d = xn.shape
    nj, ni = gw // tn, t // tm
    assert sum(rcs) == tm and all(rc % CHUNK == 0 for rc in rcs)
    assert tn % cc == 0 and cc % HEAD_DIM == 0 and gw % tn == 0 and t % tm == 0
    assert d % ni == 0 and w_out.shape[0] % (nj * ni) == 0
    off_u = col0 // tn
    kern = functools.partial(_gmlp_kernel, rcs, cc)
    next_col = lambda p: jnp.minimum(p, nj - 1)
    wo_spec = _pass_slab_spec(w_out.shape[0], w_out.shape[1], nj, ni)
    return pl.pallas_call(
        kern,
        grid=(nj + 1, ni),
        in_specs=[pl.BlockSpec((tm, d), lambda p, i: (_pass_row_tile(p, i), 0)),
                  pl.BlockSpec((d // ni, tn), lambda p, i: (i, off_u + next_col(p))),
                  pl.BlockSpec((d // ni, tn), lambda p, i: (i, off_u + nj + next_col(p))),
                  _WHOLE_VMEM, _WHOLE_VMEM, _WHOLE_VMEM,
                  wo_spec],
        out_specs=[pl.BlockSpec((tm, tn),
                                lambda p, i: (_pass_row_tile(p, i), _pass_col_tile(p))),
                   _ssq_spec(tm, _pass_row_tile, _pass_col_tile),
                   wo_spec],
        out_shape=[jax.ShapeDtypeStruct((t, gw), jnp.bfloat16),
                   jax.ShapeDtypeStruct((nj, t, LANES), jnp.float32),
                   jax.ShapeDtypeStruct(w_out.shape, jnp.bfloat16)],
        scratch_shapes=[pltpu.VMEM((2, d, tn), jnp.bfloat16),
                        pltpu.VMEM((2, d, tn), jnp.bfloat16)],
        compiler_params=_params(_ARB2),
        name="gmlp_mixer",
    )(xn, w_in, w_in, spatial_w, spatial_b, g_b.reshape(nj, 1, tn), w_out)


def _out_proj_kernel(ccs, ya_ref, yb_ref, wa_ref, wb_ref, sa_ref, sb_ref, x_ref, g_ref,
                     h_ref, xg_ref, ssq_ref):
    ya, yb = ya_ref[...], yb_ref[...]
    inv_a = _inv_rms(sa_ref, ya.shape[1])
    inv_b = _inv_rms(sb_ref, yb.shape[1])
    g = g_ref[pl.program_id(0)]
    sq = None
    for c0, cc in _chunks(ccs):
        cs = slice(c0, c0 + cc)
        h = x_ref[:, cs] + _bdot(ya, wa_ref[:, cs]) * inv_a + _bdot(yb, wb_ref[:, cs]) * inv_b
        h_ref[:, cs] = h
        xg_ref[:, cs] = (h * g[:, cs]).astype(xg_ref.dtype)
        part = _lane_group_sum(h * h)
        sq = part if sq is None else sq + part
    ssq_ref[0] = sq


def _out_proj(ya, yb, w_out, ssq_a, ssq_b, x, g_mlp, tm=512, tn=2048,
              ccs=(512, 512, 512, 256, 256)):
    t, cw = ya.shape
    gw = yb.shape[1]
    d = w_out.shape[1]
    assert cw == gw and sum(ccs) == tn and d % tn == 0 and t % tm == 0
    kern = functools.partial(_out_proj_kernel, ccs)
    ssq_in = lambda a: pl.BlockSpec((a.shape[0], tm, LANES), lambda j, i: (0, i, 0))
    tile_spec = pl.BlockSpec((tm, tn), lambda j, i: (i, j))
    return pl.pallas_call(
        kern,
        grid=(d // tn, t // tm),
        in_specs=[pl.BlockSpec((tm, cw), lambda j, i: (i, 0)),
                  pl.BlockSpec((tm, gw), lambda j, i: (i, 0)),
                  pl.BlockSpec((cw, tn), lambda j, i: (0, j), pipeline_mode=pl.Buffered(1)),
                  pl.BlockSpec((gw, tn), lambda j, i: (1, j), pipeline_mode=pl.Buffered(1)),
                  ssq_in(ssq_a), ssq_in(ssq_b), tile_spec, _WHOLE_VMEM],
        out_specs=[tile_spec, tile_spec,
                   _ssq_spec(tm, lambda j, i: i, lambda j: j)],
        out_shape=[jax.ShapeDtypeStruct((t, d), jnp.float32),
                   jax.ShapeDtypeStruct((t, d), jnp.bfloat16),
                   jax.ShapeDtypeStruct((d // tn, t, LANES), jnp.float32)],
        compiler_params=_params(_ARB2),
        name="out_proj",
    )(ya, yb, w_out, w_out, ssq_a, ssq_b, x, g_mlp.reshape(d // tn, 1, tn))


def _mlp_up_kernel(rc, x_ref, wc_ref, ssq_ref, wd_ref, r_ref, wdb_ref, wbf_a, wbf_b):
    kc = wc_ref.shape[0]
    chunk_rows = pl.ds(pl.multiple_of(pl.program_id(1) * kc, kc), kc)
    wdb_ref[...] = wd_ref[...].astype(wdb_ref.dtype)

    def step(cur_ref, nxt_ref):
        nxt_ref[chunk_rows, :] = wc_ref[...].astype(nxt_ref.dtype)
        if cur_ref is None:
            return
        inv = _inv_rms(ssq_ref, x_ref.shape[1])
        for r in range(r_ref.shape[0] // rc):
            rs = slice(r * rc, (r + 1) * rc)
            z = jnp.maximum(_bdot(x_ref[rs, :], cur_ref[...]) * inv[rs, :], 0.0)
            r_ref[rs, :] = (z * z).astype(r_ref.dtype)

    _run_streamed_pass(step, wbf_a, wbf_b)


def _mlp_up(xg, w_up, ssq, w_down, tm=1024, tn=1024, rc=512):
    t, d = xg.shape
    f = w_up.shape[1]
    nj, ni = f // tn, t // tm
    assert f % tn == 0 and t % tm == 0 and tm % rc == 0 and d % ni == 0 and f % (nj * ni) == 0
    kern = functools.partial(_mlp_up_kernel, rc)
    slab_spec = _pass_slab_spec(f, d, nj, ni)
    return pl.pallas_call(
        kern,
        grid=(nj + 1, ni),
        in_specs=[pl.BlockSpec((tm, d), lambda p, i: (_pass_row_tile(p, i), 0)),
                  pl.BlockSpec((d // ni, tn), lambda p, i: (i, jnp.minimum(p, nj - 1))),
                  pl.BlockSpec((ssq.shape[0], tm, LANES),
                               lambda p, i: (0, _pass_row_tile(p, i), 0)),
                  slab_spec],
        out_specs=[pl.BlockSpec((tm, tn),
                                lambda p, i: (_pass_row_tile(p, i), _pass_col_tile(p))),
                   slab_spec],
        out_shape=[jax.ShapeDtypeStruct((t, f), jnp.bfloat16),
                   jax.ShapeDtypeStruct(w_down.shape, jnp.bfloat16)],
        scratch_shapes=[pltpu.VMEM((d, tn), jnp.bfloat16),
                        pltpu.VMEM((d, tn), jnp.bfloat16)],
        compiler_params=_params(_ARB2),
        name="mlp_up",
    )(xg, w_up, ssq, w_down)


def _mlp_down_kernel(r_ref, w_ref, h_ref, o_ref):
    k = pl.program_id(2)
    last = pl.num_programs(2) - 1

    @pl.when(k == 0)
    def _():
        o_ref[...] = _bdot(r_ref[...], w_ref[...])

    @pl.when((k > 0) & (k < last))
    def _():
        o_ref[...] += _bdot(r_ref[...], w_ref[...])

    @pl.when(k == last)
    def _():
        o_ref[...] = h_ref[...] + (o_ref[...] + _bdot(r_ref[...], w_ref[...]))


def _mlp_down(r, w_down, h, tm=1024, tn=1024, tk=4096):
    t, f = r.shape
    d = w_down.shape[1]
    assert d % tn == 0 and t % tm == 0 and f % tk == 0
    assert f // tk >= 2
    return pl.pallas_call(
        _mlp_down_kernel,
        grid=(d // tn, t // tm, f // tk),
        in_specs=[pl.BlockSpec((tm, tk), lambda j, i, k: (i, k)),
                  pl.BlockSpec((tk, tn), lambda j, i, k: (k, j)),
                  pl.BlockSpec((tm, tn), lambda j, i, k: (i, j))],
        out_specs=pl.BlockSpec((tm, tn), lambda j, i, k: (i, j)),
        out_shape=jax.ShapeDtypeStruct((t, d), jnp.float32),
        compiler_params=_params(("arbitrary", "arbitrary", "arbitrary")),
        name="mlp_down",
    )(r, w_down, h)


def kernel(x, mix_norm_g, w_in, conv_w, spatial_w, spatial_b, conv_out_norm_g,
           gmlp_out_norm_g, w_out, mlp_norm_g, w_up, w_down, final_norm_g):
    bsz, seq, d = x.shape
    depth = w_in.shape[0]
    cw = conv_w.shape[2]
    gw = spatial_w.shape[1] * HEAD_DIM
    assert seq % CHUNK == 0 and w_in.shape[2] == 3 * cw + 2 * gw
    h = x.reshape(bsz * seq, d)
    for l in range(depth):
        xn = _rmsnorm(h, mix_norm_g[l], jnp.bfloat16)
        ya, ssq_a = _conv_mixer(xn, w_in[l], conv_w[l], conv_out_norm_g[l], seq, cw)
        yb, ssq_b, w_out_bf16 = _gmlp_mixer(xn, w_in[l], spatial_w[l], spatial_b[l],
                                            gmlp_out_norm_g[l], 3 * cw, gw, w_out[l])
        h, xg, ssq_h = _out_proj(ya, yb, w_out_bf16, ssq_a, ssq_b, h, mlp_norm_g[l])
        r, w_down_bf16 = _mlp_up(xg, w_up[l], ssq_h, w_down[l])
        h = _mlp_down(r, w_down_bf16, h)
    out = _rmsnorm(h, final_norm_g, x.dtype)
    return out.reshape(bsz, seq, d)
```

```python
import functools

import jax
import jax.numpy as jnp
from jax import lax
from jax.experimental import pallas as pl
from jax.experimental.pallas import tpu as pltpu

EPS = 1e-5
HEAD_DIM = 128
CHUNK = 128
CONV_K = 3
LANES = 128
CAST_PIECES = 4
HALO = 16

_VMEM_LIMIT = 60000 * 1024

_ARB2 = ("arbitrary", "arbitrary")


def _params(sem):
    return pltpu.CompilerParams(dimension_semantics=sem, vmem_limit_bytes=_VMEM_LIMIT)


def _bdot(a, b):
    return jnp.dot(a.astype(jnp.bfloat16), b.astype(jnp.bfloat16),
                   preferred_element_type=jnp.float32)


def _lane_group_sum(sq):
    acc = sq[:, 0:LANES]
    for c in range(1, sq.shape[1] // LANES):
        acc = acc + sq[:, c * LANES:(c + 1) * LANES]
    return acc


def _ssq_spec(tm, row_tile, col_tile):
    return pl.BlockSpec((1, tm, LANES), lambda a, i: (col_tile(a), row_tile(a, i), 0))


def _inv_rms(ssq_ref, width):
    total = jnp.sum(jnp.sum(ssq_ref[...], axis=0), axis=-1, keepdims=True)
    return lax.rsqrt(total / width + EPS)


_WHOLE_VMEM = pl.BlockSpec(memory_space=pltpu.VMEM)


def _chunks(sizes):
    out, start = [], 0
    for size in sizes:
        out.append((start, size))
        start += size
    return out


def _pass_row_tile(p, i):
    return jnp.where(p == 0, 0, i)


def _pass_col_tile(p):
    return jnp.maximum(p - 1, 0)


def _pass_slab_spec(rows, cols, nj, ni):
    return pl.BlockSpec((rows // (nj * ni), cols),
                        lambda p, i: (_pass_col_tile(p) * ni + _pass_row_tile(p, i), 0))


def _run_streamed_pass(step, wbf_a, wbf_b):
    p = pl.program_id(0)
    pl.when(p == 0)(lambda: step(None, wbf_a))
    pl.when((p > 0) & (p % 2 == 1))(lambda: step(wbf_a, wbf_b))
    pl.when((p > 0) & (p % 2 == 0))(lambda: step(wbf_b, wbf_a))


def _rmsnorm_kernel(x_ref, g_ref, o_ref):
    x = x_ref[...]
    ms = jnp.mean(x * x, axis=-1, keepdims=True)
    o_ref[...] = (x * lax.rsqrt(ms + EPS) * g_ref[...]).astype(o_ref.dtype)


def _rmsnorm(x, g, out_dtype, tr=512):
    t, d = x.shape
    assert t % tr == 0
    return pl.pallas_call(
        _rmsnorm_kernel,
        grid=(t // tr,),
        in_specs=[pl.BlockSpec((tr, d), lambda i: (i, 0)),
                  pl.BlockSpec((1, d), lambda i: (0, 0))],
        out_specs=pl.BlockSpec((tr, d), lambda i: (i, 0)),
        out_shape=jax.ShapeDtypeStruct((t, d), out_dtype),
        compiler_params=_params(("arbitrary",)),
        name="rmsnorm_cast",
    )(x, g.reshape(1, d))


def _conv_kernel(tiles_per_seq, rcs, x_ref, wb_ref, wc_ref, wh_ref, cw_ref, g_ref,
                 y_ref, ssq_ref, wbf_ref, ext_ref):
    j, i = pl.program_id(0), pl.program_id(1)
    tm, tn = y_ref.shape
    taps = [cw_ref[k, j] for k in range(CONV_K)]
    g = g_ref[j]

    @pl.when(i == 0)
    def _():
        for n, w_ref in enumerate((wb_ref, wc_ref, wh_ref)):
            wbf_ref[n] = w_ref[...].astype(wbf_ref.dtype)

    @pl.when(i % tiles_per_seq == 0)
    def _():
        ext_ref[0:HALO, :] = jnp.zeros((HALO, tn), jnp.float32)

    for r0, rc in _chunks(rcs):
        rs = slice(r0, r0 + rc)
        xb = x_ref[rs, :]
        bg = _bdot(xb, wbf_ref[0])
        ch = _bdot(xb, wbf_ref[1]) * _bdot(xb, wbf_ref[2])
        base = HALO + r0
        ext_ref[base:base + rc, :] = ch
        conv = taps[CONV_K - 1] * ch
        for k in range(CONV_K - 1):
            shift = CONV_K - 1 - k
            conv = conv + taps[k] * ext_ref[base - shift:base - shift + rc, :]
        ya = bg * conv
        y_ref[rs, :] = (ya * g).astype(y_ref.dtype)
        ssq_ref[0, rs, :] = _lane_group_sum(ya * ya)
    ext_ref[0:HALO, :] = ext_ref[tm:tm + HALO, :]


def _conv_mixer(xn, w_in, conv_w, g_a, seq, cw, tm=1024, tn=256,
                rcs=(256, 256, 256, 128, 128)):
    t, d = xn.shape
    nj = cw // tn
    assert sum(rcs) == tm and seq % tm == 0 and t % tm == 0 and cw % tn == 0
    kern = functools.partial(_conv_kernel, seq // tm, rcs)
    wspec = lambda off: pl.BlockSpec((d, tn), lambda j, i, off=off: (0, j + off))
    return pl.pallas_call(
        kern,
        grid=(nj, t // tm),
        in_specs=[pl.BlockSpec((tm, d), lambda j, i: (i, 0)),
                  wspec(0), wspec(nj), wspec(2 * nj), _WHOLE_VMEM, _WHOLE_VMEM],
        out_specs=[pl.BlockSpec((tm, tn), lambda j, i: (i, j)),
                   _ssq_spec(tm, lambda j, i: i, lambda j: j)],
        out_shape=[jax.ShapeDtypeStruct((t, cw), jnp.bfloat16),
                   jax.ShapeDtypeStruct((nj, t, LANES), jnp.float32)],
        scratch_shapes=[pltpu.VMEM((3, d, tn), jnp.bfloat16),
                        pltpu.VMEM((HALO + tm, tn), jnp.float32)],
        compiler_params=_params(_ARB2),
        name="conv_mixer",
    )(xn, w_in, w_in, w_in, conv_w.reshape(CONV_K, nj, 1, tn), g_a.reshape(nj, 1, tn))


def _gmlp_kernel(rcs, cc, x_ref, wuc_ref, wvc_ref, sw_ref, sb_ref, g_ref, wo_ref,
                 y_ref, ssq_ref, wob_ref, wbf_a, wbf_b):
    tm, tn = y_ref.shape
    kc = wuc_ref.shape[0]
    chunk_row0 = pl.multiple_of(pl.program_id(1) * kc, kc)
    kp, sp = kc // CAST_PIECES, wo_ref.shape[0] // CAST_PIECES

    def cast_piece(nxt_ref, n):
        for m, wc_ref in enumerate((wuc_ref, wvc_ref)):
            nxt_ref[m, pl.ds(chunk_row0 + n * kp, kp), :] = (
                wc_ref[n * kp:(n + 1) * kp, :].astype(nxt_ref.dtype))
        wob_ref[n * sp:(n + 1) * sp, :] = wo_ref[n * sp:(n + 1) * sp, :].astype(wob_ref.dtype)

    def step(cur_ref, nxt_ref):
        if cur_ref is None:
            for n in range(CAST_PIECES):
                cast_piece(nxt_ref, n)
            return
        col_tile = pl.program_id(0) - 1
        row = lax.broadcasted_iota(jnp.int32, (CHUNK, CHUNK), 0)
        col = lax.broadcasted_iota(jnp.int32, (CHUNK, CHUNK), 1)
        causal = col <= row
        heads = tn // HEAD_DIM
        head0 = col_tile * heads
        ws = [jnp.where(causal, sw_ref[head0 + h], 0.0).astype(jnp.bfloat16)
              for h in range(heads)]
        bias = [jnp.broadcast_to(sb_ref[pl.ds(head0 + h, 1), :], (HEAD_DIM, CHUNK)).T
                for h in range(heads)]
        g = g_ref[col_tile]
        units = [(r0, rc, c0) for r0, rc in _chunks(rcs) for c0 in range(0, tn, cc)]
        sub_heads = cc // HEAD_DIM

        def uv_dots(r0, rc, c0):
            xb = x_ref[r0:r0 + rc, :]
            return (_bdot(xb, cur_ref[0, :, c0:c0 + cc]),
                    _bdot(xb, cur_ref[1, :, c0:c0 + cc]))

        uv = uv_dots(*units[0])
        sq = None
        for n, (r0, rc, c0) in enumerate(units):
            u, v = uv
            if n + 1 < len(units):
                uv = uv_dots(*units[n + 1])
            gu = jax.nn.gelu(u)
            gv = jax.nn.gelu(v).astype(jnp.bfloat16)
            n_sub = rc // CHUNK
            h0 = c0 // HEAD_DIM
            gate = []
            for h in range(sub_heads):
                hs = slice(h * HEAD_DIM, (h + 1) * HEAD_DIM)
                vcat = jnp.concatenate(
                    [gv[c * CHUNK:(c + 1) * CHUNK, hs] for c in range(n_sub)], axis=1)
                gate.append(_bdot(ws[h0 + h], vcat))
            s = jnp.concatenate(
                [jnp.concatenate([gate[h][:, c * CHUNK:(c + 1) * CHUNK] + bias[h0 + h]
                                  for h in range(sub_heads)], axis=1)
                 for c in range(n_sub)], axis=0)
            yb = gu * s
            y_ref[r0:r0 + rc, c0:c0 + cc] = (yb * g[:, c0:c0 + cc]).astype(y_ref.dtype)
            part = _lane_group_sum(yb * yb)
            sq = part if c0 == 0 else sq + part
            if c0 + cc == tn:
                ssq_ref[0, r0:r0 + rc, :] = sq
            if n < CAST_PIECES:
                cast_piece(nxt_ref, n)

    _run_streamed_pass(step, wbf_a, wbf_b)


def _gmlp_mixer(xn, w_in, spatial_w, spatial_b, g_b, col0, gw, w_out, tm=1024, tn=512,
                rcs=(256, 256, 256, 128, 128), cc=256):
    t, d = xn.shape
    nj, ni = gw // tn, t // tm
    assert sum(rcs) == tm and all(rc % CHUNK == 0 for rc in rcs)
    assert tn % cc == 0 and cc % HEAD_DIM == 0 and gw % tn == 0 and t % tm == 0
    assert d % ni == 0 and w_out.shape[0] % (nj * ni) == 0
    assert len(rcs) * (tn // cc) >= CAST_PIECES
    off_u = col0 // tn
    kern = functools.partial(_gmlp_kernel, rcs, cc)
    next_col = lambda p: jnp.minimum(p, nj - 1)
    wo_spec = _pass_slab_spec(w_out.shape[0], w_out.shape[1], nj, ni)
    return pl.pallas_call(
        kern,
        grid=(nj + 1, ni),
        in_specs=[pl.BlockSpec((tm, d), lambda p, i: (_pass_row_tile(p, i), 0)),
                  pl.BlockSpec((d // ni, tn), lambda p, i: (i, off_u + next_col(p))),
                  pl.BlockSpec((d // ni, tn), lambda p, i: (i, off_u + nj + next_col(p))),
                  _WHOLE_VMEM, _WHOLE_VMEM, _WHOLE_VMEM,
                  wo_spec],
        out_specs=[pl.BlockSpec((tm, tn),
                                lambda p, i: (_pass_row_tile(p, i), _pass_col_tile(p))),
                   _ssq_spec(tm, _pass_row_tile, _pass_col_tile),
                   wo_spec],
        out_shape=[jax.ShapeDtypeStruct((t, gw), jnp.bfloat16),
                   jax.ShapeDtypeStruct((nj, t, LANES), jnp.float32),
                   jax.ShapeDtypeStruct(w_out.shape, jnp.bfloat16)],
        scratch_shapes=[pltpu.VMEM((2, d, tn), jnp.bfloat16),
                        pltpu.VMEM((2, d, tn), jnp.bfloat16)],
        compiler_params=_params(_ARB2),
        name="gmlp_mixer",
    )(xn, w_in, w_in, spatial_w, spatial_b, g_b.reshape(nj, 1, tn), w_out)


def _out_proj_kernel(ccs, ya_ref, yb_ref, wa_ref, wb_ref, sa_ref, sb_ref, x_ref, g_ref,
                     h_ref, xg_ref, ssq_ref):
    ya, yb = ya_ref[...], yb_ref[...]
    inv_a = _inv_rms(sa_ref, ya.shape[1])
    inv_b = _inv_rms(sb_ref, yb.shape[1])
    g = g_ref[pl.program_id(0)]
    sq = None
    for c0, cc in _chunks(ccs):
        cs = slice(c0, c0 + cc)
        h = x_ref[:, cs] + _bdot(ya, wa_ref[:, cs]) * inv_a + _bdot(yb, wb_ref[:, cs]) * inv_b
        h_ref[:, cs] = h
        xg_ref[:, cs] = (h * g[:, cs]).astype(xg_ref.dtype)
        part = _lane_group_sum(h * h)
        sq = part if sq is None else sq + part
    ssq_ref[0] = sq


def _out_proj(ya, yb, w_out, ssq_a, ssq_b, x, g_mlp, tm=512, tn=2048,
              ccs=(512, 512, 512, 256, 256)):
    t, cw = ya.shape
    gw = yb.shape[1]
    d = w_out.shape[1]
    assert cw == gw and sum(ccs) == tn and d % tn == 0 and t % tm == 0
    kern = functools.partial(_out_proj_kernel, ccs)
    ssq_in = lambda a: pl.BlockSpec((a.shape[0], tm, LANES), lambda j, i: (0, i, 0))
    tile_spec = pl.BlockSpec((tm, tn), lambda j, i: (i, j))
    return pl.pallas_call(
        kern,
        grid=(d // tn, t // tm),
        in_specs=[pl.BlockSpec((tm, cw), lambda j, i: (i, 0)),
                  pl.BlockSpec((tm, gw), lambda j, i: (i, 0)),
                  pl.BlockSpec((cw, tn), lambda j, i: (0, j), pipeline_mode=pl.Buffered(1)),
                  pl.BlockSpec((gw, tn), lambda j, i: (1, j), pipeline_mode=pl.Buffered(1)),
                  ssq_in(ssq_a), ssq_in(ssq_b), tile_spec, _WHOLE_VMEM],
        out_specs=[tile_spec, tile_spec,
                   _ssq_spec(tm, lambda j, i: i, lambda j: j)],
        out_shape=[jax.ShapeDtypeStruct((t, d), jnp.float32),
                   jax.ShapeDtypeStruct((t, d), jnp.bfloat16),
                   jax.ShapeDtypeStruct((d // tn, t, LANES), jnp.float32)],
        compiler_params=_params(_ARB2),
        name="out_proj",
    )(ya, yb, w_out, w_out, ssq_a, ssq_b, x, g_mlp.reshape(d // tn, 1, tn))


def _mlp_up_kernel(rc, x_ref, wc_ref, ssq_ref, wd_ref, r_ref, wdb_ref, wbf_a, wbf_b):
    kc = wc_ref.shape[0]
    chunk_row0 = pl.multiple_of(pl.program_id(1) * kc, kc)
    n_rc = r_ref.shape[0] // rc
    kp, sp = kc // n_rc, wd_ref.shape[0] // n_rc

    def cast_piece(nxt_ref, n):
        nxt_ref[pl.ds(chunk_row0 + n * kp, kp), :] = (
            wc_ref[n * kp:(n + 1) * kp, :].astype(nxt_ref.dtype))
        wdb_ref[n * sp:(n + 1) * sp, :] = wd_ref[n * sp:(n + 1) * sp, :].astype(wdb_ref.dtype)

    def step(cur_ref, nxt_ref):
        if cur_ref is None:
            for n in range(n_rc):
                cast_piece(nxt_ref, n)
            return
        inv = _inv_rms(ssq_ref, x_ref.shape[1])
        for r in range(n_rc):
            rs = slice(r * rc, (r + 1) * rc)
            z = jnp.maximum(_bdot(x_ref[rs, :], cur_ref[...]) * inv[rs, :], 0.0)
            r_ref[rs, :] = (z * z).astype(r_ref.dtype)
            cast_piece(nxt_ref, r)

    _run_streamed_pass(step, wbf_a, wbf_b)


def _mlp_up(xg, w_up, ssq, w_down, tm=1024, tn=1024, rc=256):
    t, d = xg.shape
    f = w_up.shape[1]
    nj, ni = f // tn, t // tm
    assert f % tn == 0 and t % tm == 0 and tm % rc == 0 and d % ni == 0 and f % (nj * ni) == 0
    kern = functools.partial(_mlp_up_kernel, rc)
    slab_spec = _pass_slab_spec(f, d, nj, ni)
    return pl.pallas_call(
        kern,
        grid=(nj + 1, ni),
        in_specs=[pl.BlockSpec((tm, d), lambda p, i: (_pass_row_tile(p, i), 0)),
                  pl.BlockSpec((d // ni, tn), lambda p, i: (i, jnp.minimum(p, nj - 1))),
                  pl.BlockSpec((ssq.shape[0], tm, LANES),
                               lambda p, i: (0, _pass_row_tile(p, i), 0)),
                  slab_spec],
        out_specs=[pl.BlockSpec((tm, tn),
                                lambda p, i: (_pass_row_tile(p, i), _pass_col_tile(p))),
                   slab_spec],
        out_shape=[jax.ShapeDtypeStruct((t, f), jnp.bfloat16),
                   jax.ShapeDtypeStruct(w_down.shape, jnp.bfloat16)],
        scratch_shapes=[pltpu.VMEM((d, tn), jnp.bfloat16),
                        pltpu.VMEM((d, tn), jnp.bfloat16)],
        compiler_params=_params(_ARB2),
        name="mlp_up",
    )(xg, w_up, ssq, w_down)


def _mlp_down_kernel(r_ref, w_ref, h_ref, o_ref):
    k = pl.program_id(2)
    last = pl.num_programs(2) - 1

    @pl.when(k == 0)
    def _():
        o_ref[...] = _bdot(r_ref[...], w_ref[...])

    @pl.when((k > 0) & (k < last))
    def _():
        o_ref[...] += _bdot(r_ref[...], w_ref[...])

    @pl.when(k == last)
    def _():
        o_ref[...] = h_ref[...] + (o_ref[...] + _bdot(r_ref[...], w_ref[...]))


def _mlp_down(r, w_down, h, tm=1024, tn=1024, tk=4096):
    t, f = r.shape
    d = w_down.shape[1]
    assert d % tn == 0 and t % tm == 0 and f % tk == 0
    assert f // tk >= 2
    return pl.pallas_call(
        _mlp_down_kernel,
        grid=(d // tn, t // tm, f // tk),
        in_specs=[pl.BlockSpec((tm, tk), lambda j, i, k: (i, k)),
                  pl.BlockSpec((tk, tn), lambda j, i, k: (k, j)),
                  pl.BlockSpec((tm, tn), lambda j, i, k: (i, j))],
        out_specs=pl.BlockSpec((tm, tn), lambda j, i, k: (i, j)),
        out_shape=jax.ShapeDtypeStruct((t, d), jnp.float32),
        compiler_params=_params(("arbitrary", "arbitrary", "arbitrary")),
        name="mlp_down",
    )(r, w_down, h)


def kernel(x, mix_norm_g, w_in, conv_w, spatial_w, spatial_b, conv_out_norm_g,
           gmlp_out_norm_g, w_out, mlp_norm_g, w_up, w_down, final_norm_g):
    bsz, seq, d = x.shape
    depth = w_in.shape[0]
    cw = conv_w.shape[2]
    gw = spatial_w.shape[1] * HEAD_DIM
    assert seq % CHUNK == 0 and w_in.shape[2] == 3 * cw + 2 * gw
    h = x.reshape(bsz * seq, d)
    for l in range(depth):
        xn = _rmsnorm(h, mix_norm_g[l], jnp.bfloat16)
        ya, ssq_a = _conv_mixer(xn, w_in[l], conv_w[l], conv_out_norm_g[l], seq, cw)
        yb, ssq_b, w_out_bf16 = _gmlp_mixer(xn, w_in[l], spatial_w[l], spatial_b[l],
                                            gmlp_out_norm_g[l], 3 * cw, gw, w_out[l])
        h, xg, ssq_h = _out_proj(ya, yb, w_out_bf16, ssq_a, ssq_b, h, mlp_norm_g[l])
        r, w_down_bf16 = _mlp_up(xg, w_up[l], ssq_h, w_down[l])
        h = _mlp_down(r, w_down_bf16, h)
    out = _rmsnorm(h, final_norm_g, x.dtype)
    return out.reshape(bsz, seq, d)
```

```python
import functools

import jax
import jax.numpy as jnp
from jax import lax
from jax.experimental import pallas as pl
from jax.experimental.pallas import tpu as pltpu

EPS = 1e-5
HEAD_DIM = 128
CHUNK = 128
CONV_K = 3
LANES = 128
CAST_PIECES = 4
HALO = 16

_VMEM_LIMIT = 60000 * 1024

_ARB2 = ("arbitrary", "arbitrary")


def _params(sem):
    return pltpu.CompilerParams(dimension_semantics=sem, vmem_limit_bytes=_VMEM_LIMIT)


def _bdot(a, b):
    return jnp.dot(a.astype(jnp.bfloat16), b.astype(jnp.bfloat16),
                   preferred_element_type=jnp.float32)


def _lane_group_sum(sq):
    acc = sq[:, 0:LANES]
    for c in range(1, sq.shape[1] // LANES):
        acc = acc + sq[:, c * LANES:(c + 1) * LANES]
    return acc


def _ssq_spec(tm, row_tile, col_tile):
    return pl.BlockSpec((1, tm, LANES), lambda a, i: (col_tile(a), row_tile(a, i), 0))


def _inv_rms(ssq_ref, width):
    total = jnp.sum(jnp.sum(ssq_ref[...], axis=0), axis=-1, keepdims=True)
    return lax.rsqrt(total / width + EPS)


_WHOLE_VMEM = pl.BlockSpec(memory_space=pltpu.VMEM)


def _chunks(sizes):
    out, start = [], 0
    for size in sizes:
        out.append((start, size))
        start += size
    return out


def _pass_row_tile(p, i):
    return jnp.where(p == 0, 0, i)


def _pass_col_tile(p):
    return jnp.maximum(p - 1, 0)


def _pass_slab_spec(rows, cols, nj, ni):
    return pl.BlockSpec((rows // (nj * ni), cols),
                        lambda p, i: (_pass_col_tile(p) * ni + _pass_row_tile(p, i), 0))


def _run_streamed_pass(step, wbf_a, wbf_b):
    p = pl.program_id(0)
    pl.when(p == 0)(lambda: step(None, wbf_a))
    pl.when((p > 0) & (p % 2 == 1))(lambda: step(wbf_a, wbf_b))
    pl.when((p > 0) & (p % 2 == 0))(lambda: step(wbf_b, wbf_a))


def _rmsnorm_kernel(x_ref, g_ref, o_ref):
    x = x_ref[...]
    ms = jnp.mean(x * x, axis=-1, keepdims=True)
    o_ref[...] = (x * lax.rsqrt(ms + EPS) * g_ref[...]).astype(o_ref.dtype)


def _rmsnorm(x, g, out_dtype, tr=512):
    t, d = x.shape
    assert t % tr == 0
    return pl.pallas_call(
        _rmsnorm_kernel,
        grid=(t // tr,),
        in_specs=[pl.BlockSpec((tr, d), lambda i: (i, 0)),
                  pl.BlockSpec((1, d), lambda i: (0, 0))],
        out_specs=pl.BlockSpec((tr, d), lambda i: (i, 0)),
        out_shape=jax.ShapeDtypeStruct((t, d), out_dtype),
        compiler_params=_params(("arbitrary",)),
        name="rmsnorm_cast",
    )(x, g.reshape(1, d))


def _conv_kernel(tiles_per_seq, rcs, x_ref, wbc_ref, wcc_ref, whc_ref, cw_ref, g_ref,
                 y_ref, ssq_ref, wbf_a, wbf_b, ext_ref):
    tm, tn = y_ref.shape
    kc = wbc_ref.shape[0]
    chunk_row0 = pl.multiple_of(pl.program_id(1) * kc, kc)
    kp = kc // CAST_PIECES

    def cast_piece(nxt_ref, n):
        for m, wc_ref in enumerate((wbc_ref, wcc_ref, whc_ref)):
            nxt_ref[m, pl.ds(chunk_row0 + n * kp, kp), :] = (
                wc_ref[n * kp:(n + 1) * kp, :].astype(nxt_ref.dtype))

    def step(cur_ref, nxt_ref):
        if cur_ref is None:
            for n in range(CAST_PIECES):
                cast_piece(nxt_ref, n)
            ext_ref[0:HALO, :] = jnp.zeros((HALO, tn), jnp.float32)
            return
        col_tile = pl.program_id(0) - 1
        taps = [cw_ref[k, col_tile] for k in range(CONV_K)]
        g = g_ref[col_tile]
        for n, (r0, rc) in enumerate(_chunks(rcs)):
            rs = slice(r0, r0 + rc)
            xb = x_ref[rs, :]
            bg = _bdot(xb, cur_ref[0])
            ch = _bdot(xb, cur_ref[1]) * _bdot(xb, cur_ref[2])
            base = HALO + r0
            ext_ref[base:base + rc, :] = ch
            conv = taps[CONV_K - 1] * ch
            for k in range(CONV_K - 1):
                shift = CONV_K - 1 - k
                conv = conv + taps[k] * ext_ref[base - shift:base - shift + rc, :]
            ya = bg * conv
            y_ref[rs, :] = (ya * g).astype(y_ref.dtype)
            ssq_ref[0, rs, :] = _lane_group_sum(ya * ya)
            if n < CAST_PIECES:
                cast_piece(nxt_ref, n)
        next_starts_sequence = (pl.program_id(1) + 1) % tiles_per_seq == 0
        ext_ref[0:HALO, :] = jnp.where(next_starts_sequence, 0.0, ext_ref[tm:tm + HALO, :])

    _run_streamed_pass(step, wbf_a, wbf_b)


def _conv_mixer(xn, w_in, conv_w, g_a, seq, cw, tm=1024, tn=256,
                rcs=(256, 256, 256, 128, 128)):
    t, d = xn.shape
    nj, ni = cw // tn, t // tm
    assert sum(rcs) == tm and seq % tm == 0 and t % tm == 0 and cw % tn == 0
    assert d % ni == 0 and len(rcs) >= CAST_PIECES
    kern = functools.partial(_conv_kernel, seq // tm, rcs)
    next_col = lambda p: jnp.minimum(p, nj - 1)
    wspec = lambda off: pl.BlockSpec((d // ni, tn),
                                     lambda p, i, off=off: (i, off + next_col(p)))
    return pl.pallas_call(
        kern,
        grid=(nj + 1, ni),
        in_specs=[pl.BlockSpec((tm, d), lambda p, i: (_pass_row_tile(p, i), 0)),
                  wspec(0), wspec(nj), wspec(2 * nj), _WHOLE_VMEM, _WHOLE_VMEM],
        out_specs=[pl.BlockSpec((tm, tn),
                                lambda p, i: (_pass_row_tile(p, i), _pass_col_tile(p))),
                   _ssq_spec(tm, _pass_row_tile, _pass_col_tile)],
        out_shape=[jax.ShapeDtypeStruct((t, cw), jnp.bfloat16),
                   jax.ShapeDtypeStruct((nj, t, LANES), jnp.float32)],
        scratch_shapes=[pltpu.VMEM((3, d, tn), jnp.bfloat16),
                        pltpu.VMEM((3, d, tn), jnp.bfloat16),
                        pltpu.VMEM((HALO + tm, tn), jnp.float32)],
        compiler_params=_params(_ARB2),
        name="conv_mixer",
    )(xn, w_in, w_in, w_in, conv_w.reshape(CONV_K, nj, 1, tn), g_a.reshape(nj, 1, tn))


def _gmlp_kernel(rcs, cc, x_ref, wuc_ref, wvc_ref, sw_ref, sb_ref, g_ref, wo_ref,
                 y_ref, ssq_ref, wob_ref, wbf_a, wbf_b):
    tm, tn = y_ref.shape
    kc = wuc_ref.shape[0]
    chunk_row0 = pl.multiple_of(pl.program_id(1) * kc, kc)
    kp, sp = kc // CAST_PIECES, wo_ref.shape[0] // CAST_PIECES

    def cast_piece(nxt_ref, n):
        for m, wc_ref in enumerate((wuc_ref, wvc_ref)):
            nxt_ref[m, pl.ds(chunk_row0 + n * kp, kp), :] = (
                wc_ref[n * kp:(n + 1) * kp, :].astype(nxt_ref.dtype))
        wob_ref[n * sp:(n + 1) * sp, :] = wo_ref[n * sp:(n + 1) * sp, :].astype(wob_ref.dtype)

    def step(cur_ref, nxt_ref):
        if cur_ref is None:
            for n in range(CAST_PIECES):
                cast_piece(nxt_ref, n)
            return
        col_tile = pl.program_id(0) - 1
        row = lax.broadcasted_iota(jnp.int32, (CHUNK, CHUNK), 0)
        col = lax.broadcasted_iota(jnp.int32, (CHUNK, CHUNK), 1)
        causal = col <= row
        heads = tn // HEAD_DIM
        head0 = col_tile * heads
        ws = [jnp.where(causal, sw_ref[head0 + h], 0.0).astype(jnp.bfloat16)
              for h in range(heads)]
        bias = [jnp.broadcast_to(sb_ref[pl.ds(head0 + h, 1), :], (HEAD_DIM, CHUNK)).T
                for h in range(heads)]
        g = g_ref[col_tile]
        units = [(r0, rc, c0) for r0, rc in _chunks(rcs) for c0 in range(0, tn, cc)]
        sub_heads = cc // HEAD_DIM

        def uv_dots(r0, rc, c0):
            xb = x_ref[r0:r0 + rc, :]
            return (_bdot(xb, cur_ref[0, :, c0:c0 + cc]),
                    _bdot(xb, cur_ref[1, :, c0:c0 + cc]))

        uv = uv_dots(*units[0])
        sq = None
        for n, (r0, rc, c0) in enumerate(units):
            u, v = uv
            if n + 1 < len(units):
                uv = uv_dots(*units[n + 1])
            gu = jax.nn.gelu(u)
            gv = jax.nn.gelu(v).astype(jnp.bfloat16)
            n_sub = rc // CHUNK
            h0 = c0 // HEAD_DIM
            gate = []
            for h in range(sub_heads):
                hs = slice(h * HEAD_DIM, (h + 1) * HEAD_DIM)
                vcat = jnp.concatenate(
                    [gv[c * CHUNK:(c + 1) * CHUNK, hs] for c in range(n_sub)], axis=1)
                gate.append(_bdot(ws[h0 + h], vcat))
            s = jnp.concatenate(
                [jnp.concatenate([gate[h][:, c * CHUNK:(c + 1) * CHUNK] + bias[h0 + h]
                                  for h in range(sub_heads)], axis=1)
                 for c in range(n_sub)], axis=0)
            yb = gu * s
            y_ref[r0:r0 + rc, c0:c0 + cc] = (yb * g[:, c0:c0 + cc]).astype(y_ref.dtype)
            part = _lane_group_sum(yb * yb)
            sq = part if c0 == 0 else sq + part
            if c0 + cc == tn:
                ssq_ref[0, r0:r0 + rc, :] = sq
            if n < CAST_PIECES:
                cast_piece(nxt_ref, n)

    _run_streamed_pass(step, wbf_a, wbf_b)


def _gmlp_mixer(xn, w_in, spatial_w, spatial_b, g_b, col0, gw, w_out, tm=1024, tn=512,
                rcs=(256, 256, 256, 128, 128), cc=256):
    t, d = xn.shape
    nj, ni = gw // tn, t // tm
    assert sum(rcs) == tm and all(rc % CHUNK == 0 for rc in rcs)
    assert tn % cc == 0 and cc % HEAD_DIM == 0 and gw % tn == 0 and t % tm == 0
    assert d % ni == 0 and w_out.shape[0] % (nj * ni) == 0
    assert len(rcs) * (tn // cc) >= CAST_PIECES
    off_u = col0 // tn
    kern = functools.partial(_gmlp_kernel, rcs, cc)
    next_col = lambda p: jnp.minimum(p, nj - 1)
    wo_spec = _pass_slab_spec(w_out.shape[0], w_out.shape[1], nj, ni)
    return pl.pallas_call(
        kern,
        grid=(nj + 1, ni),
        in_specs=[pl.BlockSpec((tm, d), lambda p, i: (_pass_row_tile(p, i), 0)),
                  pl.BlockSpec((d // ni, tn), lambda p, i: (i, off_u + next_col(p))),
                  pl.BlockSpec((d // ni, tn), lambda p, i: (i, off_u + nj + next_col(p))),
                  _WHOLE_VMEM, _WHOLE_VMEM, _WHOLE_VMEM,
                  wo_spec],
        out_specs=[pl.BlockSpec((tm, tn),
                                lambda p, i: (_pass_row_tile(p, i), _pass_col_tile(p))),
                   _ssq_spec(tm, _pass_row_tile, _pass_col_tile),
                   wo_spec],
        out_shape=[jax.ShapeDtypeStruct((t, gw), jnp.bfloat16),
                   jax.ShapeDtypeStruct((nj, t, LANES), jnp.float32),
                   jax.ShapeDtypeStruct(w_out.shape, jnp.bfloat16)],
        scratch_shapes=[pltpu.VMEM((2, d, tn), jnp.bfloat16),
                        pltpu.VMEM((2, d, tn), jnp.bfloat16)],
        compiler_params=_params(_ARB2),
        name="gmlp_mixer",
    )(xn, w_in, w_in, spatial_w, spatial_b, g_b.reshape(nj, 1, tn), w_out)


def _out_proj_kernel(ccs, ya_ref, yb_ref, wa_ref, wb_ref, sa_ref, sb_ref, x_ref, g_ref,
                     h_ref, xg_ref, ssq_ref):
    ya, yb = ya_ref[...], yb_ref[...]
    inv_a = _inv_rms(sa_ref, ya.shape[1])
    inv_b = _inv_rms(sb_ref, yb.shape[1])
    g = g_ref[pl.program_id(0)]
    sq = None
    for c0, cc in _chunks(ccs):
        cs = slice(c0, c0 + cc)
        h = x_ref[:, cs] + _bdot(ya, wa_ref[:, cs]) * inv_a + _bdot(yb, wb_ref[:, cs]) * inv_b
        h_ref[:, cs] = h
        xg_ref[:, cs] = (h * g[:, cs]).astype(xg_ref.dtype)
        part = _lane_group_sum(h * h)
        sq = part if sq is None else sq + part
    ssq_ref[0] = sq


def _out_proj(ya, yb, w_out, ssq_a, ssq_b, x, g_mlp, tm=512, tn=2048,
              ccs=(512, 512, 512, 256, 256)):
    t, cw = ya.shape
    gw = yb.shape[1]
    d = w_out.shape[1]
    assert cw == gw and sum(ccs) == tn and d % tn == 0 and t % tm == 0
    kern = functools.partial(_out_proj_kernel, ccs)
    ssq_in = lambda a: pl.BlockSpec((a.shape[0], tm, LANES), lambda j, i: (0, i, 0))
    tile_spec = pl.BlockSpec((tm, tn), lambda j, i: (i, j))
    return pl.pallas_call(
        kern,
        grid=(d // tn, t // tm),
        in_specs=[pl.BlockSpec((tm, cw), lambda j, i: (i, 0)),
                  pl.BlockSpec((tm, gw), lambda j, i: (i, 0)),
                  pl.BlockSpec((cw, tn), lambda j, i: (0, j), pipeline_mode=pl.Buffered(1)),
                  pl.BlockSpec((gw, tn), lambda j, i: (1, j), pipeline_mode=pl.Buffered(1)),
                  ssq_in(ssq_a), ssq_in(ssq_b), tile_spec, _WHOLE_VMEM],
        out_specs=[tile_spec, tile_spec,
                   _ssq_spec(tm, lambda j, i: i, lambda j: j)],
        out_shape=[jax.ShapeDtypeStruct((t, d), jnp.float32),
                   jax.ShapeDtypeStruct((t, d), jnp.bfloat16),
                   jax.ShapeDtypeStruct((d // tn, t, LANES), jnp.float32)],
        compiler_params=_params(_ARB2),
        name="out_proj",
    )(ya, yb, w_out, w_out, ssq_a, ssq_b, x, g_mlp.reshape(d // tn, 1, tn))


def _mlp_up_kernel(rc, x_ref, wc_ref, ssq_ref, wd_ref, r_ref, wdb_ref, wbf_a, wbf_b):
    kc = wc_ref.shape[0]
    chunk_row0 = pl.multiple_of(pl.program_id(1) * kc, kc)
    n_rc = r_ref.shape[0] // rc
    kp, sp = kc // n_rc, wd_ref.shape[0] // n_rc

    def cast_piece(nxt_ref, n):
        nxt_ref[pl.ds(chunk_row0 + n * kp, kp), :] = (
            wc_ref[n * kp:(n + 1) * kp, :].astype(nxt_ref.dtype))
        wdb_ref[n * sp:(n + 1) * sp, :] = wd_ref[n * sp:(n + 1) * sp, :].astype(wdb_ref.dtype)

    def step(cur_ref, nxt_ref):
        if cur_ref is None:
            for n in range(n_rc):
                cast_piece(nxt_ref, n)
            return
        inv = _inv_rms(ssq_ref, x_ref.shape[1])
        for r in range(n_rc):
            rs = slice(r * rc, (r + 1) * rc)
            z = jnp.maximum(_bdot(x_ref[rs, :], cur_ref[...]) * inv[rs, :], 0.0)
            r_ref[rs, :] = (z * z).astype(r_ref.dtype)
            cast_piece(nxt_ref, r)

    _run_streamed_pass(step, wbf_a, wbf_b)


def _mlp_up(xg, w_up, ssq, w_down, tm=1024, tn=1024, rc=256):
    t, d = xg.shape
    f = w_up.shape[1]
    nj, ni = f // tn, t // tm
    assert f % tn == 0 and t % tm == 0 and tm % rc == 0 and d % ni == 0 and f % (nj * ni) == 0
    kern = functools.partial(_mlp_up_kernel, rc)
    slab_spec = _pass_slab_spec(f, d, nj, ni)
    return pl.pallas_call(
        kern,
        grid=(nj + 1, ni),
        in_specs=[pl.BlockSpec((tm, d), lambda p, i: (_pass_row_tile(p, i), 0)),
                  pl.BlockSpec((d // ni, tn), lambda p, i: (i, jnp.minimum(p, nj - 1))),
                  pl.BlockSpec((ssq.shape[0], tm, LANES),
                               lambda p, i: (0, _pass_row_tile(p, i), 0)),
                  slab_spec],
        out_specs=[pl.BlockSpec((tm, tn),
                                lambda p, i: (_pass_row_tile(p, i), _pass_col_tile(p))),
                   slab_spec],
        out_shape=[jax.ShapeDtypeStruct((t, f), jnp.bfloat16),
                   jax.ShapeDtypeStruct(w_down.shape, jnp.bfloat16)],
        scratch_shapes=[pltpu.VMEM((d, tn), jnp.bfloat16),
                        pltpu.VMEM((d, tn), jnp.bfloat16)],
        compiler_params=_params(_ARB2),
        name="mlp_up",
    )(xg, w_up, ssq, w_down)


def _mlp_down_kernel(r_ref, w_ref, h_ref, o_ref):
    k = pl.program_id(2)
    last = pl.num_programs(2) - 1

    @pl.when(k == 0)
    def _():
        o_ref[...] = _bdot(r_ref[...], w_ref[...])

    @pl.when((k > 0) & (k < last))
    def _():
        o_ref[...] += _bdot(r_ref[...], w_ref[...])

    @pl.when(k == last)
    def _():
        o_ref[...] = h_ref[...] + (o_ref[...] + _bdot(r_ref[...], w_ref[...]))


def _mlp_down(r, w_down, h, tm=1024, tn=1024, tk=4096):
    t, f = r.shape
    d = w_down.shape[1]
    assert d % tn == 0 and t % tm == 0 and f % tk == 0
    assert f // tk >= 2
    return pl.pallas_call(
        _mlp_down_kernel,
        grid=(d // tn, t // tm, f // tk),
        in_specs=[pl.BlockSpec((tm, tk), lambda j, i, k: (i, k)),
                  pl.BlockSpec((tk, tn), lambda j, i, k: (k, j)),
                  pl.BlockSpec((tm, tn), lambda j, i, k: (i, j))],
        out_specs=pl.BlockSpec((tm, tn), lambda j, i, k: (i, j)),
        out_shape=jax.ShapeDtypeStruct((t, d), jnp.float32),
        compiler_params=_params(("arbitrary", "arbitrary", "arbitrary")),
        name="mlp_down",
    )(r, w_down, h)


def kernel(x, mix_norm_g, w_in, conv_w, spatial_w, spatial_b, conv_out_norm_g,
           gmlp_out_norm_g, w_out, mlp_norm_g, w_up, w_down, final_norm_g):
    bsz, seq, d = x.shape
    depth = w_in.shape[0]
    cw = conv_w.shape[2]
    gw = spatial_w.shape[1] * HEAD_DIM
    assert seq % CHUNK == 0 and w_in.shape[2] == 3 * cw + 2 * gw
    h = x.reshape(bsz * seq, d)
    for l in range(depth):
        xn = _rmsnorm(h, mix_norm_g[l], jnp.bfloat16)
        ya, ssq_a = _conv_mixer(xn, w_in[l], conv_w[l], conv_out_norm_g[l], seq, cw)
        yb, ssq_b, w_out_bf16 = _gmlp_mixer(xn, w_in[l], spatial_w[l], spatial_b[l],
                                            gmlp_out_norm_g[l], 3 * cw, gw, w_out[l])
        h, xg, ssq_h = _out_proj(ya, yb, w_out_bf16, ssq_a, ssq_b, h, mlp_norm_g[l])
        r, w_down_bf16 = _mlp_up(xg, w_up[l], ssq_h, w_down[l])
        h = _mlp_down(r, w_down_bf16, h)
    out = _rmsnorm(h, final_norm_g, x.dtype)
    return out.reshape(bsz, seq, d)
```

```python
import functools

import jax
import jax.numpy as jnp
from jax import lax
from jax.experimental import pallas as pl
from jax.experimental.pallas import tpu as pltpu

EPS = 1e-5
HEAD_DIM = 128
CHUNK = 128
CONV_K = 3
LANES = 128
CAST_PIECES = 4
HALO = 16

_VMEM_LIMIT = 60000 * 1024

_ARB2 = ("arbitrary", "arbitrary")


def _params(sem):
    return pltpu.CompilerParams(dimension_semantics=sem, vmem_limit_bytes=_VMEM_LIMIT)


def _bdot(a, b):
    return jnp.dot(a.astype(jnp.bfloat16), b.astype(jnp.bfloat16),
                   preferred_element_type=jnp.float32)


def _lane_group_sum(sq):
    acc = sq[:, 0:LANES]
    for c in range(1, sq.shape[1] // LANES):
        acc = acc + sq[:, c * LANES:(c + 1) * LANES]
    return acc


def _ssq_spec(tm, row_tile, col_tile):
    return pl.BlockSpec((1, tm, LANES), lambda a, i: (col_tile(a), row_tile(a, i), 0))


def _inv_rms(ssq_ref, width):
    total = jnp.sum(jnp.sum(ssq_ref[...], axis=0), axis=-1, keepdims=True)
    return lax.rsqrt(total / width + EPS)


_WHOLE_VMEM = pl.BlockSpec(memory_space=pltpu.VMEM)


def _chunks(sizes):
    out, start = [], 0
    for size in sizes:
        out.append((start, size))
        start += size
    return out


def _pass_row_tile(p, i):
    return jnp.where(p == 0, 0, i)


def _pass_col_tile(p):
    return jnp.maximum(p - 1, 0)


def _pass_slab_spec(rows, cols, nj, ni):
    return pl.BlockSpec((rows // (nj * ni), cols),
                        lambda p, i: (_pass_col_tile(p) * ni + _pass_row_tile(p, i), 0))


def _run_streamed_pass(step, wbf_a, wbf_b):
    p = pl.program_id(0)
    pl.when(p == 0)(lambda: step(None, wbf_a))
    pl.when((p > 0) & (p % 2 == 1))(lambda: step(wbf_a, wbf_b))
    pl.when((p > 0) & (p % 2 == 0))(lambda: step(wbf_b, wbf_a))


def _rmsnorm_kernel(x_ref, g_ref, o_ref):
    x = x_ref[...]
    ms = jnp.mean(x * x, axis=-1, keepdims=True)
    o_ref[...] = (x * lax.rsqrt(ms + EPS) * g_ref[...]).astype(o_ref.dtype)


def _rmsnorm(x, g, out_dtype, tr=512):
    t, d = x.shape
    assert t % tr == 0
    return pl.pallas_call(
        _rmsnorm_kernel,
        grid=(t // tr,),
        in_specs=[pl.BlockSpec((tr, d), lambda i: (i, 0)),
                  pl.BlockSpec((1, d), lambda i: (0, 0))],
        out_specs=pl.BlockSpec((tr, d), lambda i: (i, 0)),
        out_shape=jax.ShapeDtypeStruct((t, d), out_dtype),
        compiler_params=_params(("arbitrary",)),
        name="rmsnorm_cast",
    )(x, g.reshape(1, d))


def _conv_kernel(tiles_per_seq, rcs, x_ref, wb_ref, wc_ref, wh_ref, cw_ref, g_ref,
                 y_ref, ssq_ref, wbf_ref, ext_ref):
    j, i = pl.program_id(0), pl.program_id(1)
    tm, tn = y_ref.shape
    taps = [cw_ref[k, j] for k in range(CONV_K)]
    g = g_ref[j]

    @pl.when(i == 0)
    def _():
        for n, w_ref in enumerate((wb_ref, wc_ref, wh_ref)):
            wbf_ref[n] = w_ref[...].astype(wbf_ref.dtype)

    @pl.when(i % tiles_per_seq == 0)
    def _():
        ext_ref[0:HALO, :] = jnp.zeros((HALO, tn), jnp.float32)

    for r0, rc in _chunks(rcs):
        rs = slice(r0, r0 + rc)
        xb = x_ref[rs, :]
        bg = _bdot(xb, wbf_ref[0])
        ch = _bdot(xb, wbf_ref[1]) * _bdot(xb, wbf_ref[2])
        base = HALO + r0
        ext_ref[base:base + rc, :] = ch
        conv = taps[CONV_K - 1] * ch
        for k in range(CONV_K - 1):
            shift = CONV_K - 1 - k
            conv = conv + taps[k] * ext_ref[base - shift:base - shift + rc, :]
        ya = bg * conv
        y_ref[rs, :] = (ya * g).astype(y_ref.dtype)
        ssq_ref[0, rs, :] = _lane_group_sum(ya * ya)
    ext_ref[0:HALO, :] = ext_ref[tm:tm + HALO, :]


def _conv_mixer(xn, w_in, conv_w, g_a, seq, cw, tm=1024, tn=256,
                rcs=(256, 256, 256, 128, 128)):
    t, d = xn.shape
    nj = cw // tn
    assert sum(rcs) == tm and seq % tm == 0 and t % tm == 0 and cw % tn == 0
    kern = functools.partial(_conv_kernel, seq // tm, rcs)
    wspec = lambda off: pl.BlockSpec((d, tn), lambda j, i, off=off: (0, j + off))
    return pl.pallas_call(
        kern,
        grid=(nj, t // tm),
        in_specs=[pl.BlockSpec((tm, d), lambda j, i: (i, 0)),
                  wspec(0), wspec(nj), wspec(2 * nj), _WHOLE_VMEM, _WHOLE_VMEM],
        out_specs=[pl.BlockSpec((tm, tn), lambda j, i: (i, j)),
                   _ssq_spec(tm, lambda j, i: i, lambda j: j)],
        out_shape=[jax.ShapeDtypeStruct((t, cw), jnp.bfloat16),
                   jax.ShapeDtypeStruct((nj, t, LANES), jnp.float32)],
        scratch_shapes=[pltpu.VMEM((3, d, tn), jnp.bfloat16),
                        pltpu.VMEM((HALO + tm, tn), jnp.float32)],
        compiler_params=_params(_ARB2),
        name="conv_mixer",
    )(xn, w_in, w_in, w_in, conv_w.reshape(CONV_K, nj, 1, tn), g_a.reshape(nj, 1, tn))


def _gmlp_kernel(rcs, cc, x_ref, wuc_ref, wvc_ref, sw_ref, sb_ref, g_ref, wo_ref,
                 y_ref, ssq_ref, wob_ref, wbf_a, wbf_b):
    tm, tn = y_ref.shape
    kc = wuc_ref.shape[0]
    chunk_row0 = pl.multiple_of(pl.program_id(1) * kc, kc)
    kp, sp = kc // CAST_PIECES, wo_ref.shape[0] // CAST_PIECES

    def cast_piece(nxt_ref, n):
        for m, wc_ref in enumerate((wuc_ref, wvc_ref)):
            nxt_ref[m, pl.ds(chunk_row0 + n * kp, kp), :] = (
                wc_ref[n * kp:(n + 1) * kp, :].astype(nxt_ref.dtype))
        wob_ref[n * sp:(n + 1) * sp, :] = wo_ref[n * sp:(n + 1) * sp, :].astype(wob_ref.dtype)

    def step(cur_ref, nxt_ref):
        if cur_ref is None:
            for n in range(CAST_PIECES):
                cast_piece(nxt_ref, n)
            return
        col_tile = pl.program_id(0) - 1
        row = lax.broadcasted_iota(jnp.int32, (CHUNK, CHUNK), 0)
        col = lax.broadcasted_iota(jnp.int32, (CHUNK, CHUNK), 1)
        causal = col <= row
        heads = tn // HEAD_DIM
        head0 = col_tile * heads
        ws = [jnp.where(causal, sw_ref[head0 + h], 0.0).astype(jnp.bfloat16)
              for h in range(heads)]
        bias = [jnp.broadcast_to(sb_ref[pl.ds(head0 + h, 1), :], (HEAD_DIM, CHUNK)).T
                for h in range(heads)]
        g = g_ref[col_tile]
        units = [(r0, rc, c0) for r0, rc in _chunks(rcs) for c0 in range(0, tn, cc)]
        sub_heads = cc // HEAD_DIM

        def uv_dots(r0, rc, c0):
            xb = x_ref[r0:r0 + rc, :]
            return (_bdot(xb, cur_ref[0, :, c0:c0 + cc]),
                    _bdot(xb, cur_ref[1, :, c0:c0 + cc]))

        uv = uv_dots(*units[0])
        sq = None
        for n, (r0, rc, c0) in enumerate(units):
            u, v = uv
            if n + 1 < len(units):
                uv = uv_dots(*units[n + 1])
            gu = jax.nn.gelu(u)
            gv = jax.nn.gelu(v).astype(jnp.bfloat16)
            n_sub = rc // CHUNK
            h0 = c0 // HEAD_DIM
            gate = []
            for h in range(sub_heads):
                hs = slice(h * HEAD_DIM, (h + 1) * HEAD_DIM)
                vcat = jnp.concatenate(
                    [gv[c * CHUNK:(c + 1) * CHUNK, hs] for c in range(n_sub)], axis=1)
                gate.append(_bdot(ws[h0 + h], vcat))
            s = jnp.concatenate(
                [jnp.concatenate([gate[h][:, c * CHUNK:(c + 1) * CHUNK] + bias[h0 + h]
                                  for h in range(sub_heads)], axis=1)
                 for c in range(n_sub)], axis=0)
            yb = gu * s
            y_ref[r0:r0 + rc, c0:c0 + cc] = (yb * g[:, c0:c0 + cc]).astype(y_ref.dtype)
            part = _lane_group_sum(yb * yb)
            sq = part if c0 == 0 else sq + part
            if c0 + cc == tn:
                ssq_ref[0, r0:r0 + rc, :] = sq
            if n < CAST_PIECES:
                cast_piece(nxt_ref, n)

    _run_streamed_pass(step, wbf_a, wbf_b)


def _gmlp_mixer(xn, w_in, spatial_w, spatial_b, g_b, col0, gw, w_out, tm=1024, tn=512,
                rcs=(256, 256, 256, 128, 128), cc=256):
    t, d = xn.shape
    nj, ni = gw // tn, t // tm
    assert sum(rcs) == tm and all(rc % CHUNK == 0 for rc in rcs)
    assert tn % cc == 0 and cc % HEAD_DIM == 0 and gw % tn == 0 and t % tm == 0
    assert d % ni == 0 and w_out.shape[0] % (nj * ni) == 0
    assert len(rcs) * (tn // cc) >= CAST_PIECES
    off_u = col0 // tn
    kern = functools.partial(_gmlp_kernel, rcs, cc)
    next_col = lambda p: jnp.minimum(p, nj - 1)
    wo_spec = _pass_slab_spec(w_out.shape[0], w_out.shape[1], nj, ni)
    return pl.pallas_call(
        kern,
        grid=(nj + 1, ni),
        in_specs=[pl.BlockSpec((tm, d), lambda p, i: (_pass_row_tile(p, i), 0)),
                  pl.BlockSpec((d // ni, tn), lambda p, i: (i, off_u + next_col(p))),
                  pl.BlockSpec((d // ni, tn), lambda p, i: (i, off_u + nj + next_col(p))),
                  _WHOLE_VMEM, _WHOLE_VMEM, _WHOLE_VMEM,
                  wo_spec],
        out_specs=[pl.BlockSpec((tm, tn),
                                lambda p, i: (_pass_row_tile(p, i), _pass_col_tile(p))),
                   _ssq_spec(tm, _pass_row_tile, _pass_col_tile),
                   wo_spec],
        out_shape=[jax.ShapeDtypeStruct((t, gw), jnp.bfloat16),
                   jax.ShapeDtypeStruct((nj, t, LANES), jnp.float32),
                   jax.ShapeDtypeStruct(w_out.shape, jnp.bfloat16)],
        scratch_shapes=[pltpu.VMEM((2, d, tn), jnp.bfloat16),
                        pltpu.VMEM((2, d, tn), jnp.bfloat16)],
        compiler_params=_params(_ARB2),
        name="gmlp_mixer",
    )(xn, w_in, w_in, spatial_w, spatial_b, g_b.reshape(nj, 1, tn), w_out)


def _out_proj_kernel(ccs, ya_ref, yb_ref, wa_ref, wb_ref, sa_ref, sb_ref, x_ref, g_ref,
                     h_ref, xg_ref, ssq_ref):
    ya, yb = ya_ref[...], yb_ref[...]
    inv_a = _inv_rms(sa_ref, ya.shape[1])
    inv_b = _inv_rms(sb_ref, yb.shape[1])
    g = g_ref[pl.program_id(0)]
    sq = None
    for c0, cc in _chunks(ccs):
        cs = slice(c0, c0 + cc)
        h = x_ref[:, cs] + _bdot(ya, wa_ref[:, cs]) * inv_a + _bdot(yb, wb_ref[:, cs]) * inv_b
        h_ref[:, cs] = h
        xg_ref[:, cs] = (h * g[:, cs]).astype(xg_ref.dtype)
        part = _lane_group_sum(h * h)
        sq = part if sq is None else sq + part
    ssq_ref[0] = sq


def _out_proj(ya, yb, w_out, ssq_a, ssq_b, x, g_mlp, tm=512, tn=2048,
              ccs=(512, 512, 512, 256, 256)):
    t, cw = ya.shape
    gw = yb.shape[1]
    d = w_out.shape[1]
    assert cw == gw and sum(ccs) == tn and d % tn == 0 and t % tm == 0
    kern = functools.partial(_out_proj_kernel, ccs)
    ssq_in = lambda a: pl.BlockSpec((a.shape[0], tm, LANES), lambda j, i: (0, i, 0))
    tile_spec = pl.BlockSpec((tm, tn), lambda j, i: (i, j))
    return pl.pallas_call(
        kern,
        grid=(d // tn, t // tm),
        in_specs=[pl.BlockSpec((tm, cw), lambda j, i: (i, 0)),
                  pl.BlockSpec((tm, gw), lambda j, i: (i, 0)),
                  pl.BlockSpec((cw, tn), lambda j, i: (0, j), pipeline_mode=pl.Buffered(1)),
                  pl.BlockSpec((gw, tn), lambda j, i: (1, j), pipeline_mode=pl.Buffered(1)),
                  ssq_in(ssq_a), ssq_in(ssq_b), tile_spec, _WHOLE_VMEM],
        out_specs=[tile_spec, tile_spec,
                   _ssq_spec(tm, lambda j, i: i, lambda j: j)],
        out_shape=[jax.ShapeDtypeStruct((t, d), jnp.float32),
                   jax.ShapeDtypeStruct((t, d), jnp.bfloat16),
                   jax.ShapeDtypeStruct((d // tn, t, LANES), jnp.float32)],
        compiler_params=_params(_ARB2),
        name="out_proj",
    )(ya, yb, w_out, w_out, ssq_a, ssq_b, x, g_mlp.reshape(d // tn, 1, tn))


def _mlp_up_kernel(rc, x_ref, wc_ref, ssq_ref, wd_ref, r_ref, wdb_ref, wbf_a, wbf_b):
    kc = wc_ref.shape[0]
    chunk_row0 = pl.multiple_of(pl.program_id(1) * kc, kc)
    n_rc = r_ref.shape[0] // rc
    kp, sp = kc // n_rc, wd_ref.shape[0] // n_rc

    def cast_piece(nxt_ref, n):
        nxt_ref[pl.ds(chunk_row0 + n * kp, kp), :] = (
            wc_ref[n * kp:(n + 1) * kp, :].astype(nxt_ref.dtype))
        wdb_ref[n * sp:(n + 1) * sp, :] = wd_ref[n * sp:(n + 1) * sp, :].astype(wdb_ref.dtype)

    def step(cur_ref, nxt_ref):
        if cur_ref is None:
            for n in range(n_rc):
                cast_piece(nxt_ref, n)
            return
        inv = _inv_rms(ssq_ref, x_ref.shape[1])
        for r in range(n_rc):
            rs = slice(r * rc, (r + 1) * rc)
            z = jnp.maximum(_bdot(x_ref[rs, :], cur_ref[...]) * inv[rs, :], 0.0)
            r_ref[rs, :] = (z * z).astype(r_ref.dtype)
            cast_piece(nxt_ref, r)

    _run_streamed_pass(step, wbf_a, wbf_b)


def _mlp_up(xg, w_up, ssq, w_down, tm=1024, tn=1024, rc=256):
    t, d = xg.shape
    f = w_up.shape[1]
    nj, ni = f // tn, t // tm
    assert f % tn == 0 and t % tm == 0 and tm % rc == 0 and d % ni == 0 and f % (nj * ni) == 0
    kern = functools.partial(_mlp_up_kernel, rc)
    slab_spec = _pass_slab_spec(f, d, nj, ni)
    return pl.pallas_call(
        kern,
        grid=(nj + 1, ni),
        in_specs=[pl.BlockSpec((tm, d), lambda p, i: (_pass_row_tile(p, i), 0)),
                  pl.BlockSpec((d // ni, tn), lambda p, i: (i, jnp.minimum(p, nj - 1))),
                  pl.BlockSpec((ssq.shape[0], tm, LANES),
                               lambda p, i: (0, _pass_row_tile(p, i), 0)),
                  slab_spec],
        out_specs=[pl.BlockSpec((tm, tn),
                                lambda p, i: (_pass_row_tile(p, i), _pass_col_tile(p))),
                   slab_spec],
        out_shape=[jax.ShapeDtypeStruct((t, f), jnp.bfloat16),
                   jax.ShapeDtypeStruct(w_down.shape, jnp.bfloat16)],
        scratch_shapes=[pltpu.VMEM((d, tn), jnp.bfloat16),
                        pltpu.VMEM((d, tn), jnp.bfloat16)],
        compiler_params=_params(_ARB2),
        name="mlp_up",
    )(xg, w_up, ssq, w_down)


def _mlp_down_kernel(r_ref, w_ref, h_ref, o_ref):
    k = pl.program_id(2)
    last = pl.num_programs(2) - 1

    @pl.when(k == 0)
    def _():
        o_ref[...] = _bdot(r_ref[...], w_ref[...])

    @pl.when((k > 0) & (k < last))
    def _():
        o_ref[...] += _bdot(r_ref[...], w_ref[...])

    @pl.when(k == last)
    def _():
        o_ref[...] = h_ref[...] + (o_ref[...] + _bdot(r_ref[...], w_ref[...]))


def _mlp_down(r, w_down, h, tm=1024, tn=1024, tk=4096):
    t, f = r.shape
    d = w_down.shape[1]
    assert d % tn == 0 and t % tm == 0 and f % tk == 0
    assert f // tk >= 2
    return pl.pallas_call(
        _mlp_down_kernel,
        grid=(d // tn, t // tm, f // tk),
        in_specs=[pl.BlockSpec((tm, tk), lambda j, i, k: (i, k)),
                  pl.BlockSpec((tk, tn), lambda j, i, k: (k, j)),
                  pl.BlockSpec((tm, tn), lambda j, i, k: (i, j))],
        out_specs=pl.BlockSpec((tm, tn), lambda j, i, k: (i, j)),
        out_shape=jax.ShapeDtypeStruct((t, d), jnp.float32),
        compiler_params=_params(("arbitrary", "arbitrary", "arbitrary")),
        name="mlp_down",
    )(r, w_down, h)


def kernel(x, mix_norm_g, w_in, conv_w, spatial_w, spatial_b, conv_out_norm_g,
           gmlp_out_norm_g, w_out, mlp_norm_g, w_up, w_down, final_norm_g):
    bsz, seq, d = x.shape
    depth = w_in.shape[0]
    cw = conv_w.shape[2]
    gw = spatial_w.shape[1] * HEAD_DIM
    assert seq % CHUNK == 0 and w_in.shape[2] == 3 * cw + 2 * gw
    h = x.reshape(bsz * seq, d)
    for l in range(depth):
        xn = _rmsnorm(h, mix_norm_g[l], jnp.bfloat16)
        yb, ssq_b, w_out_bf16 = _gmlp_mixer(xn, w_in[l], spatial_w[l], spatial_b[l],
                                            gmlp_out_norm_g[l], 3 * cw, gw, w_out[l])
        ya, ssq_a = _conv_mixer(xn, w_in[l], conv_w[l], conv_out_norm_g[l], seq, cw)
        h, xg, ssq_h = _out_proj(ya, yb, w_out_bf16, ssq_a, ssq_b, h, mlp_norm_g[l])
        r, w_down_bf16 = _mlp_up(xg, w_up[l], ssq_h, w_down[l])
        h = _mlp_down(r, w_down_bf16, h)
    out = _rmsnorm(h, final_norm_g, x.dtype)
    return out.reshape(bsz, seq, d)
```

```python
import functools

import jax
import jax.numpy as jnp
from jax import lax
from jax.experimental import pallas as pl
from jax.experimental.pallas import tpu as pltpu

EPS = 1e-5
HEAD_DIM = 128
CHUNK = 128
CONV_K = 3
LANES = 128
CAST_PIECES = 4
HALO = 16

_VMEM_LIMIT = 60000 * 1024

_ARB2 = ("arbitrary", "arbitrary")


def _params(sem):
    return pltpu.CompilerParams(dimension_semantics=sem, vmem_limit_bytes=_VMEM_LIMIT)


def _bdot(a, b):
    return jnp.dot(a.astype(jnp.bfloat16), b.astype(jnp.bfloat16),
                   preferred_element_type=jnp.float32)


def _lane_group_sum(sq):
    acc = sq[:, 0:LANES]
    for c in range(1, sq.shape[1] // LANES):
        acc = acc + sq[:, c * LANES:(c + 1) * LANES]
    return acc


def _ssq_spec(tm, row_tile, col_tile):
    return pl.BlockSpec((1, tm, LANES), lambda a, i: (col_tile(a), row_tile(a, i), 0))


def _inv_rms(ssq_ref, width):
    total = jnp.sum(jnp.sum(ssq_ref[...], axis=0), axis=-1, keepdims=True)
    return lax.rsqrt(total / width + EPS)


_WHOLE_VMEM = pl.BlockSpec(memory_space=pltpu.VMEM)


def _chunks(sizes):
    out, start = [], 0
    for size in sizes:
        out.append((start, size))
        start += size
    return out


def _pass_row_tile(p, i):
    return jnp.where(p == 0, 0, i)


def _pass_col_tile(p):
    return jnp.maximum(p - 1, 0)


def _pass_slab_spec(rows, cols, nj, ni):
    return pl.BlockSpec((rows // (nj * ni), cols),
                        lambda p, i: (_pass_col_tile(p) * ni + _pass_row_tile(p, i), 0))


def _run_streamed_pass(step, wbf_a, wbf_b):
    p = pl.program_id(0)
    pl.when(p == 0)(lambda: step(None, wbf_a))
    pl.when((p > 0) & (p % 2 == 1))(lambda: step(wbf_a, wbf_b))
    pl.when((p > 0) & (p % 2 == 0))(lambda: step(wbf_b, wbf_a))


def _rmsnorm_kernel(x_ref, g_ref, o_ref):
    x = x_ref[...]
    ms = jnp.mean(x * x, axis=-1, keepdims=True)
    o_ref[...] = (x * lax.rsqrt(ms + EPS) * g_ref[...]).astype(o_ref.dtype)


def _rmsnorm(x, g, out_dtype, tr=256):
    t, d = x.shape
    assert t % tr == 0
    return pl.pallas_call(
        _rmsnorm_kernel,
        grid=(t // tr,),
        in_specs=[pl.BlockSpec((tr, d), lambda i: (i, 0)),
                  pl.BlockSpec((1, d), lambda i: (0, 0))],
        out_specs=pl.BlockSpec((tr, d), lambda i: (i, 0)),
        out_shape=jax.ShapeDtypeStruct((t, d), out_dtype),
        compiler_params=_params(("arbitrary",)),
        name="rmsnorm_cast",
    )(x, g.reshape(1, d))


def _conv_kernel(tiles_per_seq, rcs, x_ref, wb_ref, wc_ref, wh_ref, cw_ref, g_ref,
                 y_ref, ssq_ref, wbf_ref, ext_ref):
    j, i = pl.program_id(0), pl.program_id(1)
    tm, tn = y_ref.shape
    taps = [cw_ref[k, j] for k in range(CONV_K)]
    g = g_ref[j]

    @pl.when(i == 0)
    def _():
        for n, w_ref in enumerate((wb_ref, wc_ref, wh_ref)):
            wbf_ref[n] = w_ref[...].astype(wbf_ref.dtype)

    @pl.when(i % tiles_per_seq == 0)
    def _():
        ext_ref[0:HALO, :] = jnp.zeros((HALO, tn), jnp.float32)

    for r0, rc in _chunks(rcs):
        rs = slice(r0, r0 + rc)
        xb = x_ref[rs, :]
        bg = _bdot(xb, wbf_ref[0])
        ch = _bdot(xb, wbf_ref[1]) * _bdot(xb, wbf_ref[2])
        base = HALO + r0
        ext_ref[base:base + rc, :] = ch
        conv = taps[CONV_K - 1] * ch
        for k in range(CONV_K - 1):
            shift = CONV_K - 1 - k
            conv = conv + taps[k] * ext_ref[base - shift:base - shift + rc, :]
        ya = bg * conv
        y_ref[rs, :] = (ya * g).astype(y_ref.dtype)
        ssq_ref[0, rs, :] = _lane_group_sum(ya * ya)
    ext_ref[0:HALO, :] = ext_ref[tm:tm + HALO, :]


def _conv_mixer(xn, w_in, conv_w, g_a, seq, cw, tm=1024, tn=256,
                rcs=(256, 256, 256, 128, 128)):
    t, d = xn.shape
    nj = cw // tn
    assert sum(rcs) == tm and seq % tm == 0 and t % tm == 0 and cw % tn == 0
    kern = functools.partial(_conv_kernel, seq // tm, rcs)
    wspec = lambda off: pl.BlockSpec((d, tn), lambda j, i, off=off: (0, j + off))
    return pl.pallas_call(
        kern,
        grid=(nj, t // tm),
        in_specs=[pl.BlockSpec((tm, d), lambda j, i: (i, 0)),
                  wspec(0), wspec(nj), wspec(2 * nj), _WHOLE_VMEM, _WHOLE_VMEM],
        out_specs=[pl.BlockSpec((tm, tn), lambda j, i: (i, j)),
                   _ssq_spec(tm, lambda j, i: i, lambda j: j)],
        out_shape=[jax.ShapeDtypeStruct((t, cw), jnp.bfloat16),
                   jax.ShapeDtypeStruct((nj, t, LANES), jnp.float32)],
        scratch_shapes=[pltpu.VMEM((3, d, tn), jnp.bfloat16),
                        pltpu.VMEM((HALO + tm, tn), jnp.float32)],
        compiler_params=_params(_ARB2),
        name="conv_mixer",
    )(xn, w_in, w_in, w_in, conv_w.reshape(CONV_K, nj, 1, tn), g_a.reshape(nj, 1, tn))


def _gmlp_kernel(rcs, cc, x_ref, wuc_ref, wvc_ref, sw_ref, sb_ref, g_ref, wo_ref,
                 y_ref, ssq_ref, wob_ref, wbf_a, wbf_b):
    tm, tn = y_ref.shape
    kc = wuc_ref.shape[0]
    chunk_row0 = pl.multiple_of(pl.program_id(1) * kc, kc)
    kp, sp = kc // CAST_PIECES, wo_ref.shape[0] // CAST_PIECES

    def cast_piece(nxt_ref, n):
        for m, wc_ref in enumerate((wuc_ref, wvc_ref)):
            nxt_ref[m, pl.ds(chunk_row0 + n * kp, kp), :] = (
                wc_ref[n * kp:(n + 1) * kp, :].astype(nxt_ref.dtype))
        wob_ref[n * sp:(n + 1) * sp, :] = wo_ref[n * sp:(n + 1) * sp, :].astype(wob_ref.dtype)

    def step(cur_ref, nxt_ref):
        if cur_ref is None:
            for n in range(CAST_PIECES):
                cast_piece(nxt_ref, n)
            return
        col_tile = pl.program_id(0) - 1
        row = lax.broadcasted_iota(jnp.int32, (CHUNK, CHUNK), 0)
        col = lax.broadcasted_iota(jnp.int32, (CHUNK, CHUNK), 1)
        causal = col <= row
        heads = tn // HEAD_DIM
        head0 = col_tile * heads
        ws = [jnp.where(causal, sw_ref[head0 + h], 0.0).astype(jnp.bfloat16)
              for h in range(heads)]
        bias = [jnp.broadcast_to(sb_ref[pl.ds(head0 + h, 1), :], (HEAD_DIM, CHUNK)).T
                for h in range(heads)]
        g = g_ref[col_tile]
        units = [(r0, rc, c0) for r0, rc in _chunks(rcs) for c0 in range(0, tn, cc)]
        sub_heads = cc // HEAD_DIM

        def uv_dots(r0, rc, c0):
            xb = x_ref[r0:r0 + rc, :]
            return (_bdot(xb, cur_ref[0, :, c0:c0 + cc]),
                    _bdot(xb, cur_ref[1, :, c0:c0 + cc]))

        uv = uv_dots(*units[0])
        sq = None
        for n, (r0, rc, c0) in enumerate(units):
            u, v = uv
            if n + 1 < len(units):
                uv = uv_dots(*units[n + 1])
            gu = jax.nn.gelu(u)
            gv = jax.nn.gelu(v).astype(jnp.bfloat16)
            n_sub = rc // CHUNK
            h0 = c0 // HEAD_DIM
            gate = []
            for h in range(sub_heads):
                hs = slice(h * HEAD_DIM, (h + 1) * HEAD_DIM)
                vcat = jnp.concatenate(
                    [gv[c * CHUNK:(c + 1) * CHUNK, hs] for c in range(n_sub)], axis=1)
                gate.append(_bdot(ws[h0 + h], vcat))
            s = jnp.concatenate(
                [jnp.concatenate([gate[h][:, c * CHUNK:(c + 1) * CHUNK] + bias[h0 + h]
                                  for h in range(sub_heads)], axis=1)
                 for c in range(n_sub)], axis=0)
            yb = gu * s
            y_ref[r0:r0 + rc, c0:c0 + cc] = (yb * g[:, c0:c0 + cc]).astype(y_ref.dtype)
            part = _lane_group_sum(yb * yb)
            sq = part if c0 == 0 else sq + part
            if c0 + cc == tn:
                ssq_ref[0, r0:r0 + rc, :] = sq
            if n < CAST_PIECES:
                cast_piece(nxt_ref, n)

    _run_streamed_pass(step, wbf_a, wbf_b)


def _gmlp_mixer(xn, w_in, spatial_w, spatial_b, g_b, col0, gw, w_out, tm=1024, tn=512,
                rcs=(256, 256, 256, 128, 128), cc=256):
    t, d = xn.shape
    nj, ni = gw // tn, t // tm
    assert sum(rcs) == tm and all(rc % CHUNK == 0 for rc in rcs)
    assert tn % cc == 0 and cc % HEAD_DIM == 0 and gw % tn == 0 and t % tm == 0
    assert d % ni == 0 and w_out.shape[0] % (nj * ni) == 0
    assert len(rcs) * (tn // cc) >= CAST_PIECES
    off_u = col0 // tn
    kern = functools.partial(_gmlp_kernel, rcs, cc)
    next_col = lambda p: jnp.minimum(p, nj - 1)
    wo_spec = _pass_slab_spec(w_out.shape[0], w_out.shape[1], nj, ni)
    return pl.pallas_call(
        kern,
        grid=(nj + 1, ni),
        in_specs=[pl.BlockSpec((tm, d), lambda p, i: (_pass_row_tile(p, i), 0)),
                  pl.BlockSpec((d // ni, tn), lambda p, i: (i, off_u + next_col(p))),
                  pl.BlockSpec((d // ni, tn), lambda p, i: (i, off_u + nj + next_col(p))),
                  _WHOLE_VMEM, _WHOLE_VMEM, _WHOLE_VMEM,
                  wo_spec],
        out_specs=[pl.BlockSpec((tm, tn),
                                lambda p, i: (_pass_row_tile(p, i), _pass_col_tile(p))),
                   _ssq_spec(tm, _pass_row_tile, _pass_col_tile),
                   wo_spec],
        out_shape=[jax.ShapeDtypeStruct((t, gw), jnp.bfloat16),
                   jax.ShapeDtypeStruct((nj, t, LANES), jnp.float32),
                   jax.ShapeDtypeStruct(w_out.shape, jnp.bfloat16)],
        scratch_shapes=[pltpu.VMEM((2, d, tn), jnp.bfloat16),
                        pltpu.VMEM((2, d, tn), jnp.bfloat16)],
        compiler_params=_params(_ARB2),
        name="gmlp_mixer",
    )(xn, w_in, w_in, spatial_w, spatial_b, g_b.reshape(nj, 1, tn), w_out)


def _out_proj_kernel(ccs, ya_ref, yb_ref, wa_ref, wb_ref, sa_ref, sb_ref, x_ref, g_ref,
                     h_ref, xg_ref, ssq_ref):
    ya, yb = ya_ref[...], yb_ref[...]
    inv_a = _inv_rms(sa_ref, ya.shape[1])
    inv_b = _inv_rms(sb_ref, yb.shape[1])
    g = g_ref[pl.program_id(0)]
    sq = None
    for c0, cc in _chunks(ccs):
        cs = slice(c0, c0 + cc)
        h = x_ref[:, cs] + _bdot(ya, wa_ref[:, cs]) * inv_a + _bdot(yb, wb_ref[:, cs]) * inv_b
        h_ref[:, cs] = h
        xg_ref[:, cs] = (h * g[:, cs]).astype(xg_ref.dtype)
        part = _lane_group_sum(h * h)
        sq = part if sq is None else sq + part
    ssq_ref[0] = sq


def _out_proj(ya, yb, w_out, ssq_a, ssq_b, x, g_mlp, tm=512, tn=2048,
              ccs=(512, 512, 512, 256, 256)):
    t, cw = ya.shape
    gw = yb.shape[1]
    d = w_out.shape[1]
    assert cw == gw and sum(ccs) == tn and d % tn == 0 and t % tm == 0
    kern = functools.partial(_out_proj_kernel, ccs)
    ssq_in = lambda a: pl.BlockSpec((a.shape[0], tm, LANES), lambda j, i: (0, i, 0))
    tile_spec = pl.BlockSpec((tm, tn), lambda j, i: (i, j))
    return pl.pallas_call(
        kern,
        grid=(d // tn, t // tm),
        in_specs=[pl.BlockSpec((tm, cw), lambda j, i: (i, 0)),
                  pl.BlockSpec((tm, gw), lambda j, i: (i, 0)),
                  pl.BlockSpec((cw, tn), lambda j, i: (0, j), pipeline_mode=pl.Buffered(1)),
                  pl.BlockSpec((gw, tn), lambda j, i: (1, j), pipeline_mode=pl.Buffered(1)),
                  ssq_in(ssq_a), ssq_in(ssq_b), tile_spec, _WHOLE_VMEM],
        out_specs=[tile_spec, tile_spec,
                   _ssq_spec(tm, lambda j, i: i, lambda j: j)],
        out_shape=[jax.ShapeDtypeStruct((t, d), jnp.float32),
                   jax.ShapeDtypeStruct((t, d), jnp.bfloat16),
                   jax.ShapeDtypeStruct((d // tn, t, LANES), jnp.float32)],
        compiler_params=_params(_ARB2),
        name="out_proj",
    )(ya, yb, w_out, w_out, ssq_a, ssq_b, x, g_mlp.reshape(d // tn, 1, tn))


def _mlp_up_kernel(rc, x_ref, wc_ref, ssq_ref, wd_ref, r_ref, wdb_ref, wbf_a, wbf_b):
    kc = wc_ref.shape[0]
    chunk_row0 = pl.multiple_of(pl.program_id(1) * kc, kc)
    n_rc = r_ref.shape[0] // rc
    kp, sp = kc // n_rc, wd_ref.shape[0] // n_rc

    def cast_piece(nxt_ref, n):
        nxt_ref[pl.ds(chunk_row0 + n * kp, kp), :] = (
            wc_ref[n * kp:(n + 1) * kp, :].astype(nxt_ref.dtype))
        wdb_ref[n * sp:(n + 1) * sp, :] = wd_ref[n * sp:(n + 1) * sp, :].astype(wdb_ref.dtype)

    def step(cur_ref, nxt_ref):
        if cur_ref is None:
            for n in range(n_rc):
                cast_piece(nxt_ref, n)
            return
        inv = _inv_rms(ssq_ref, x_ref.shape[1])
        for r in range(n_rc):
            rs = slice(r * rc, (r + 1) * rc)
            z = jnp.maximum(_bdot(x_ref[rs, :], cur_ref[...]) * inv[rs, :], 0.0)
            r_ref[rs, :] = (z * z).astype(r_ref.dtype)
            cast_piece(nxt_ref, r)

    _run_streamed_pass(step, wbf_a, wbf_b)


def _mlp_up(xg, w_up, ssq, w_down, tm=1024, tn=1024, rc=256):
    t, d = xg.shape
    f = w_up.shape[1]
    nj, ni = f // tn, t // tm
    assert f % tn == 0 and t % tm == 0 and tm % rc == 0 and d % ni == 0 and f % (nj * ni) == 0
    kern = functools.partial(_mlp_up_kernel, rc)
    slab_spec = _pass_slab_spec(f, d, nj, ni)
    return pl.pallas_call(
        kern,
        grid=(nj + 1, ni),
        in_specs=[pl.BlockSpec((tm, d), lambda p, i: (_pass_row_tile(p, i), 0)),
                  pl.BlockSpec((d // ni, tn), lambda p, i: (i, jnp.minimum(p, nj - 1))),
                  pl.BlockSpec((ssq.shape[0], tm, LANES),
                               lambda p, i: (0, _pass_row_tile(p, i), 0)),
                  slab_spec],
        out_specs=[pl.BlockSpec((tm, tn),
                                lambda p, i: (_pass_row_tile(p, i), _pass_col_tile(p))),
                   slab_spec],
        out_shape=[jax.ShapeDtypeStruct((t, f), jnp.bfloat16),
                   jax.ShapeDtypeStruct(w_down.shape, jnp.bfloat16)],
        scratch_shapes=[pltpu.VMEM((d, tn), jnp.bfloat16),
                        pltpu.VMEM((d, tn), jnp.bfloat16)],
        compiler_params=_params(_ARB2),
        name="mlp_up",
    )(xg, w_up, ssq, w_down)


def _mlp_down_kernel(r_ref, w_ref, h_ref, o_ref):
    k = pl.program_id(2)
    last = pl.num_programs(2) - 1

    @pl.when(k == 0)
    def _():
        o_ref[...] = _bdot(r_ref[...], w_ref[...])

    @pl.when((k > 0) & (k < last))
    def _():
        o_ref[...] += _bdot(r_ref[...], w_ref[...])

    @pl.when(k == last)
    def _():
        o_ref[...] = h_ref[...] + (o_ref[...] + _bdot(r_ref[...], w_ref[...]))


def _mlp_down(r, w_down, h, tm=1024, tn=1024, tk=4096):
    t, f = r.shape
    d = w_down.shape[1]
    assert d % tn == 0 and t % tm == 0 and f % tk == 0
    assert f // tk >= 2
    return pl.pallas_call(
        _mlp_down_kernel,
        grid=(d // tn, t // tm, f // tk),
        in_specs=[pl.BlockSpec((tm, tk), lambda j, i, k: (i, k)),
                  pl.BlockSpec((tk, tn), lambda j, i, k: (k, j)),
                  pl.BlockSpec((tm, tn), lambda j, i, k: (i, j))],
        out_specs=pl.BlockSpec((tm, tn), lambda j, i, k: (i, j)),
        out_shape=jax.ShapeDtypeStruct((t, d), jnp.float32),
        compiler_params=_params(("arbitrary", "arbitrary", "arbitrary")),
        name="mlp_down",
    )(r, w_down, h)


def kernel(x, mix_norm_g, w_in, conv_w, spatial_w, spatial_b, conv_out_norm_g,
           gmlp_out_norm_g, w_out, mlp_norm_g, w_up, w_down, final_norm_g):
    bsz, seq, d = x.shape
    depth = w_in.shape[0]
    cw = conv_w.shape[2]
    gw = spatial_w.shape[1] * HEAD_DIM
    assert seq % CHUNK == 0 and w_in.shape[2] == 3 * cw + 2 * gw
    h = x.reshape(bsz * seq, d)
    for l in range(depth):
        xn = _rmsnorm(h, mix_norm_g[l], jnp.bfloat16)
        ya, ssq_a = _conv_mixer(xn, w_in[l], conv_w[l], conv_out_norm_g[l], seq, cw)
        yb, ssq_b, w_out_bf16 = _gmlp_mixer(xn, w_in[l], spatial_w[l], spatial_b[l],
                                            gmlp_out_norm_g[l], 3 * cw, gw, w_out[l])
        h, xg, ssq_h = _out_proj(ya, yb, w_out_bf16, ssq_a, ssq_b, h, mlp_norm_g[l])
        r, w_down_bf16 = _mlp_up(xg, w_up[l], ssq_h, w_down[l])
        h = _mlp_down(r, w_down_bf16, h)
    out = _rmsnorm(h, final_norm_g, x.dtype)
    return out.reshape(bsz, seq, d)
```

```python
import functools

import jax
import jax.numpy as jnp
from jax import lax
from jax.experimental import pallas as pl
from jax.experimental.pallas import tpu as pltpu

EPS = 1e-5
HEAD_DIM = 128
CHUNK = 128
CONV_K = 3
LANES = 128
CAST_PIECES = 4
HALO = 16

_VMEM_LIMIT = 60000 * 1024

_ARB2 = ("arbitrary", "arbitrary")


def _params(sem):
    return pltpu.CompilerParams(dimension_semantics=sem, vmem_limit_bytes=_VMEM_LIMIT)


def _bdot(a, b):
    return jnp.dot(a.astype(jnp.bfloat16), b.astype(jnp.bfloat16),
                   preferred_element_type=jnp.float32)


def _lane_group_sum(sq):
    acc = sq[:, 0:LANES]
    for c in range(1, sq.shape[1] // LANES):
        acc = acc + sq[:, c * LANES:(c + 1) * LANES]
    return acc


def _ssq_spec(tm, row_tile, col_tile):
    return pl.BlockSpec((1, tm, LANES), lambda a, i: (col_tile(a), row_tile(a, i), 0))


def _inv_rms(ssq_ref, width):
    total = jnp.sum(jnp.sum(ssq_ref[...], axis=0), axis=-1, keepdims=True)
    return lax.rsqrt(total / width + EPS)


_WHOLE_VMEM = pl.BlockSpec(memory_space=pltpu.VMEM)


def _chunks(sizes):
    out, start = [], 0
    for size in sizes:
        out.append((start, size))
        start += size
    return out


def _pass_row_tile(p, i):
    return jnp.where(p == 0, 0, i)


def _pass_col_tile(p):
    return jnp.maximum(p - 1, 0)


def _pass_slab_spec(rows, cols, nj, ni):
    return pl.BlockSpec((rows // (nj * ni), cols),
                        lambda p, i: (_pass_col_tile(p) * ni + _pass_row_tile(p, i), 0))


def _run_streamed_pass(step, wbf_a, wbf_b):
    p = pl.program_id(0)
    pl.when(p == 0)(lambda: step(None, wbf_a))
    pl.when((p > 0) & (p % 2 == 1))(lambda: step(wbf_a, wbf_b))
    pl.when((p > 0) & (p % 2 == 0))(lambda: step(wbf_b, wbf_a))


def _rmsnorm_kernel(x_ref, g_ref, o_ref):
    x = x_ref[...]
    ms = jnp.mean(x * x, axis=-1, keepdims=True)
    o_ref[...] = (x * lax.rsqrt(ms + EPS) * g_ref[...]).astype(o_ref.dtype)


def _rmsnorm(x, g, out_dtype, tr=512):
    t, d = x.shape
    assert t % tr == 0
    return pl.pallas_call(
        _rmsnorm_kernel,
        grid=(t // tr,),
        in_specs=[pl.BlockSpec((tr, d), lambda i: (i, 0)),
                  pl.BlockSpec((1, d), lambda i: (0, 0))],
        out_specs=pl.BlockSpec((tr, d), lambda i: (i, 0)),
        out_shape=jax.ShapeDtypeStruct((t, d), out_dtype),
        compiler_params=_params(("arbitrary",)),
        name="rmsnorm_cast",
    )(x, g.reshape(1, d))


def _conv_kernel(tiles_per_seq, rcs, x_ref, wb_ref, wc_ref, wh_ref, cw_ref, g_ref,
                 y_ref, ssq_ref, wbf_ref, ext_ref):
    j, i = pl.program_id(0), pl.program_id(1)
    tm, tn = y_ref.shape
    taps = [cw_ref[k, j] for k in range(CONV_K)]
    g = g_ref[j]

    @pl.when(i == 0)
    def _():
        for n, w_ref in enumerate((wb_ref, wc_ref, wh_ref)):
            wbf_ref[n] = w_ref[...].astype(wbf_ref.dtype)

    @pl.when(i % tiles_per_seq == 0)
    def _():
        ext_ref[0:HALO, :] = jnp.zeros((HALO, tn), jnp.float32)

    for r0, rc in _chunks(rcs):
        rs = slice(r0, r0 + rc)
        xb = x_ref[rs, :]
        bg = _bdot(xb, wbf_ref[0])
        ch = _bdot(xb, wbf_ref[1]) * _bdot(xb, wbf_ref[2])
        base = HALO + r0
        ext_ref[base:base + rc, :] = ch
        conv = taps[CONV_K - 1] * ch
        for k in range(CONV_K - 1):
            shift = CONV_K - 1 - k
            conv = conv + taps[k] * ext_ref[base - shift:base - shift + rc, :]
        ya = bg * conv
        y_ref[rs, :] = (ya * g).astype(y_ref.dtype)
        ssq_ref[0, rs, :] = _lane_group_sum(ya * ya)
    ext_ref[0:HALO, :] = ext_ref[tm:tm + HALO, :]


def _conv_mixer(xn, w_in, conv_w, g_a, seq, cw, tm=1024, tn=256,
                rcs=(256, 256, 256, 128, 128)):
    t, d = xn.shape
    nj = cw // tn
    assert sum(rcs) == tm and seq % tm == 0 and t % tm == 0 and cw % tn == 0
    kern = functools.partial(_conv_kernel, seq // tm, rcs)
    wspec = lambda off: pl.BlockSpec((d, tn), lambda j, i, off=off: (0, j + off))
    return pl.pallas_call(
        kern,
        grid=(nj, t // tm),
        in_specs=[pl.BlockSpec((tm, d), lambda j, i: (i, 0)),
                  wspec(0), wspec(nj), wspec(2 * nj), _WHOLE_VMEM, _WHOLE_VMEM],
        out_specs=[pl.BlockSpec((tm, tn), lambda j, i: (i, j)),
                   _ssq_spec(tm, lambda j, i: i, lambda j: j)],
        out_shape=[jax.ShapeDtypeStruct((t, cw), jnp.bfloat16),
                   jax.ShapeDtypeStruct((nj, t, LANES), jnp.float32)],
        scratch_shapes=[pltpu.VMEM((3, d, tn), jnp.bfloat16),
                        pltpu.VMEM((HALO + tm, tn), jnp.float32)],
        compiler_params=_params(_ARB2),
        name="conv_mixer",
    )(xn, w_in, w_in, w_in, conv_w.reshape(CONV_K, nj, 1, tn), g_a.reshape(nj, 1, tn))


def _gmlp_kernel(rcs, cc, x_ref, wuc_ref, wvc_ref, sw_ref, sb_ref, g_ref, wo_ref,
                 y_ref, ssq_ref, wob_ref, wbf_a, wbf_b):
    tm, tn = y_ref.shape
    kc = wuc_ref.shape[0]
    chunk_row0 = pl.multiple_of(pl.program_id(1) * kc, kc)
    kp, sp = kc // CAST_PIECES, wo_ref.shape[0] // CAST_PIECES

    def cast_piece(nxt_ref, n):
        for m, wc_ref in enumerate((wuc_ref, wvc_ref)):
            nxt_ref[m, pl.ds(chunk_row0 + n * kp, kp), :] = (
                wc_ref[n * kp:(n + 1) * kp, :].astype(nxt_ref.dtype))
        wob_ref[n * sp:(n + 1) * sp, :] = wo_ref[n * sp:(n + 1) * sp, :].astype(wob_ref.dtype)

    def step(cur_ref, nxt_ref):
        if cur_ref is None:
            for n in range(CAST_PIECES):
                cast_piece(nxt_ref, n)
            return
        col_tile = pl.program_id(0) - 1
        row = lax.broadcasted_iota(jnp.int32, (CHUNK, CHUNK), 0)
        col = lax.broadcasted_iota(jnp.int32, (CHUNK, CHUNK), 1)
        causal = col <= row
        heads = tn // HEAD_DIM
        head0 = col_tile * heads
        ws = [jnp.where(causal, sw_ref[head0 + h], 0.0).astype(jnp.bfloat16)
              for h in range(heads)]
        bias = [jnp.broadcast_to(sb_ref[pl.ds(head0 + h, 1), :], (HEAD_DIM, CHUNK)).T
                for h in range(heads)]
        g = g_ref[col_tile]
        units = [(r0, rc, c0) for r0, rc in _chunks(rcs) for c0 in range(0, tn, cc)]
        sub_heads = cc // HEAD_DIM

        def uv_dots(r0, rc, c0):
            xb = x_ref[r0:r0 + rc, :]
            return (_bdot(xb, cur_ref[0, :, c0:c0 + cc]),
                    _bdot(xb, cur_ref[1, :, c0:c0 + cc]))

        uv = uv_dots(*units[0])
        sq = None
        for n, (r0, rc, c0) in enumerate(units):
            u, v = uv
            if n + 1 < len(units):
                uv = uv_dots(*units[n + 1])
            gu = jax.nn.gelu(u)
            gv = jax.nn.gelu(v).astype(jnp.bfloat16)
            n_sub = rc // CHUNK
            h0 = c0 // HEAD_DIM
            gate = []
            for h in range(sub_heads):
                hs = slice(h * HEAD_DIM, (h + 1) * HEAD_DIM)
                vcat = jnp.concatenate(
                    [gv[c * CHUNK:(c + 1) * CHUNK, hs] for c in range(n_sub)], axis=1)
                gate.append(_bdot(ws[h0 + h], vcat))
            s = jnp.concatenate(
                [jnp.concatenate([gate[h][:, c * CHUNK:(c + 1) * CHUNK] + bias[h0 + h]
                                  for h in range(sub_heads)], axis=1)
                 for c in range(n_sub)], axis=0)
            yb = gu * s
            y_ref[r0:r0 + rc, c0:c0 + cc] = (yb * g[:, c0:c0 + cc]).astype(y_ref.dtype)
            part = _lane_group_sum(yb * yb)
            sq = part if c0 == 0 else sq + part
            if c0 + cc == tn:
                ssq_ref[0, r0:r0 + rc, :] = sq
            if n < CAST_PIECES:
                cast_piece(nxt_ref, n)

    _run_streamed_pass(step, wbf_a, wbf_b)


def _gmlp_mixer(xn, w_in, spatial_w, spatial_b, g_b, col0, gw, w_out, tm=1024, tn=512,
                rcs=(256, 256, 256, 128, 128), cc=256):
    t, d = xn.shape
    nj, ni = gw // tn, t // tm
    assert sum(rcs) == tm and all(rc % CHUNK == 0 for rc in rcs)
    assert tn % cc == 0 and cc % HEAD_DIM == 0 and gw % tn == 0 and t % tm == 0
    assert d % ni == 0 and w_out.shape[0] % (nj * ni) == 0
    assert len(rcs) * (tn // cc) >= CAST_PIECES
    off_u = col0 // tn
    kern = functools.partial(_gmlp_kernel, rcs, cc)
    next_col = lambda p: jnp.minimum(p, nj - 1)
    wo_spec = _pass_slab_spec(w_out.shape[0], w_out.shape[1], nj, ni)
    return pl.pallas_call(
        kern,
        grid=(nj + 1, ni),
        in_specs=[pl.BlockSpec((tm, d), lambda p, i: (_pass_row_tile(p, i), 0)),
                  pl.BlockSpec((d // ni, tn), lambda p, i: (i, off_u + next_col(p))),
                  pl.BlockSpec((d // ni, tn), lambda p, i: (i, off_u + nj + next_col(p))),
                  _WHOLE_VMEM, _WHOLE_VMEM, _WHOLE_VMEM,
                  wo_spec],
        out_specs=[pl.BlockSpec((tm, tn),
                                lambda p, i: (_pass_row_tile(p, i), _pass_col_tile(p))),
                   _ssq_spec(tm, _pass_row_tile, _pass_col_tile),
                   wo_spec],
        out_shape=[jax.ShapeDtypeStruct((t, gw), jnp.bfloat16),
                   jax.ShapeDtypeStruct((nj, t, LANES), jnp.float32),
                   jax.ShapeDtypeStruct(w_out.shape, jnp.bfloat16)],
        scratch_shapes=[pltpu.VMEM((2, d, tn), jnp.bfloat16),
                        pltpu.VMEM((2, d, tn), jnp.bfloat16)],
        compiler_params=_params(_ARB2),
        name="gmlp_mixer",
    )(xn, w_in, w_in, spatial_w, spatial_b, g_b.reshape(nj, 1, tn), w_out)


def _out_proj_kernel(ccs, ya_ref, yb_ref, wa_ref, wb_ref, sa_ref, sb_ref, x_ref, g_ref,
                     h_ref, xg_ref, ssq_ref):
    ya, yb = ya_ref[...], yb_ref[...]
    inv_a = _inv_rms(sa_ref, ya.shape[1])
    inv_b = _inv_rms(sb_ref, yb.shape[1])
    g = g_ref[pl.program_id(0)]
    sq = None
    for c0, cc in _chunks(ccs):
        cs = slice(c0, c0 + cc)
        h = x_ref[:, cs] + _bdot(ya, wa_ref[:, cs]) * inv_a + _bdot(yb, wb_ref[:, cs]) * inv_b
        h_ref[:, cs] = h
        xg_ref[:, cs] = (h * g[:, cs]).astype(xg_ref.dtype)
        part = _lane_group_sum(h * h)
        sq = part if sq is None else sq + part
    ssq_ref[0] = sq


def _out_proj(ya, yb, w_out, ssq_a, ssq_b, x, g_mlp, tm=512, tn=2048,
              ccs=(512, 512, 512, 256, 256)):
    t, cw = ya.shape
    gw = yb.shape[1]
    d = w_out.shape[1]
    assert cw == gw and sum(ccs) == tn and d % tn == 0 and t % tm == 0
    kern = functools.partial(_out_proj_kernel, ccs)
    ssq_in = lambda a: pl.BlockSpec((a.shape[0], tm, LANES), lambda j, i: (0, i, 0))
    tile_spec = pl.BlockSpec((tm, tn), lambda j, i: (i, j))
    return pl.pallas_call(
        kern,
        grid=(d // tn, t // tm),
        in_specs=[pl.BlockSpec((tm, cw), lambda j, i: (i, 0)),
                  pl.BlockSpec((tm, gw), lambda j, i: (i, 0)),
                  pl.BlockSpec((cw, tn), lambda j, i: (0, j), pipeline_mode=pl.Buffered(1)),
                  pl.BlockSpec((gw, tn), lambda j, i: (1, j), pipeline_mode=pl.Buffered(1)),
                  ssq_in(ssq_a), ssq_in(ssq_b), tile_spec, _WHOLE_VMEM],
        out_specs=[tile_spec, tile_spec,
                   _ssq_spec(tm, lambda j, i: i, lambda j: j)],
        out_shape=[jax.ShapeDtypeStruct((t, d), jnp.float32),
                   jax.ShapeDtypeStruct((t, d), jnp.bfloat16),
                   jax.ShapeDtypeStruct((d // tn, t, LANES), jnp.float32)],
        compiler_params=_params(_ARB2),
        name="out_proj",
    )(ya, yb, w_out, w_out, ssq_a, ssq_b, x, g_mlp.reshape(d // tn, 1, tn))


def _mlp_up_kernel(rc, x_ref, wc_ref, ssq_ref, wd_ref, r_ref, wdb_ref, wbf_a, wbf_b):
    kc = wc_ref.shape[0]
    chunk_row0 = pl.multiple_of(pl.program_id(1) * kc, kc)
    n_rc = r_ref.shape[0] // rc
    kp, sp = kc // n_rc, wd_ref.shape[0] // n_rc

    def cast_piece(nxt_ref, n):
        nxt_ref[pl.ds(chunk_row0 + n * kp, kp), :] = (
            wc_ref[n * kp:(n + 1) * kp, :].astype(nxt_ref.dtype))
        wdb_ref[n * sp:(n + 1) * sp, :] = wd_ref[n * sp:(n + 1) * sp, :].astype(wdb_ref.dtype)

    def step(cur_ref, nxt_ref):
        if cur_ref is None:
            for n in range(n_rc):
                cast_piece(nxt_ref, n)
            return
        inv = _inv_rms(ssq_ref, x_ref.shape[1])
        for r in range(n_rc):
            rs = slice(r * rc, (r + 1) * rc)
            z = jnp.maximum(_bdot(x_ref[rs, :], cur_ref[...]) * inv[rs, :], 0.0)
            r_ref[rs, :] = (z * z).astype(r_ref.dtype)
            cast_piece(nxt_ref, r)

    _run_streamed_pass(step, wbf_a, wbf_b)


def _mlp_up(xg, w_up, ssq, w_down, tm=1024, tn=1024, rc=256):
    t, d = xg.shape
    f = w_up.shape[1]
    nj, ni = f // tn, t // tm
    assert f % tn == 0 and t % tm == 0 and tm % rc == 0 and d % ni == 0 and f % (nj * ni) == 0
    kern = functools.partial(_mlp_up_kernel, rc)
    slab_spec = _pass_slab_spec(f, d, nj, ni)
    return pl.pallas_call(
        kern,
        grid=(nj + 1, ni),
        in_specs=[pl.BlockSpec((tm, d), lambda p, i: (_pass_row_tile(p, i), 0)),
                  pl.BlockSpec((d // ni, tn), lambda p, i: (i, jnp.minimum(p, nj - 1))),
                  pl.BlockSpec((ssq.shape[0], tm, LANES),
                               lambda p, i: (0, _pass_row_tile(p, i), 0)),
                  slab_spec],
        out_specs=[pl.BlockSpec((tm, tn),
                                lambda p, i: (_pass_row_tile(p, i), _pass_col_tile(p))),
                   slab_spec],
        out_shape=[jax.ShapeDtypeStruct((t, f), jnp.bfloat16),
                   jax.ShapeDtypeStruct(w_down.shape, jnp.bfloat16)],
        scratch_shapes=[pltpu.VMEM((d, tn), jnp.bfloat16),
                        pltpu.VMEM((d, tn), jnp.bfloat16)],
        compiler_params=_params(_ARB2),
        name="mlp_up",
    )(xg, w_up, ssq, w_down)


def _mlp_down_kernel(r_ref, w_ref, h_ref, o_ref):
    k = pl.program_id(2)
    last = pl.num_programs(2) - 1

    @pl.when(k == 0)
    def _():
        o_ref[...] = _bdot(r_ref[...], w_ref[...])

    @pl.when((k > 0) & (k < last))
    def _():
        o_ref[...] += _bdot(r_ref[...], w_ref[...])

    @pl.when(k == last)
    def _():
        o_ref[...] = h_ref[...] + (o_ref[...] + _bdot(r_ref[...], w_ref[...]))


def _mlp_down(r, w_down, h, tm=1024, tn=1024, tk=4096):
    t, f = r.shape
    d = w_down.shape[1]
    assert d % tn == 0 and t % tm == 0 and f % tk == 0
    assert f // tk >= 2
    return pl.pallas_call(
        _mlp_down_kernel,
        grid=(t // tm, d // tn, f // tk),
        in_specs=[pl.BlockSpec((tm, tk), lambda i, j, k: (i, k)),
                  pl.BlockSpec((tk, tn), lambda i, j, k: (k, j)),
                  pl.BlockSpec((tm, tn), lambda i, j, k: (i, j))],
        out_specs=pl.BlockSpec((tm, tn), lambda i, j, k: (i, j)),
        out_shape=jax.ShapeDtypeStruct((t, d), jnp.float32),
        compiler_params=_params(("arbitrary", "arbitrary", "arbitrary")),
        name="mlp_down",
    )(r, w_down, h)


def kernel(x, mix_norm_g, w_in, conv_w, spatial_w, spatial_b, conv_out_norm_g,
           gmlp_out_norm_g, w_out, mlp_norm_g, w_up, w_down, final_norm_g):
    bsz, seq, d = x.shape
    depth = w_in.shape[0]
    cw = conv_w.shape[2]
    gw = spatial_w.shape[1] * HEAD_DIM
    assert seq % CHUNK == 0 and w_in.shape[2] == 3 * cw + 2 * gw
    h = x.reshape(bsz * seq, d)
    for l in range(depth):
        xn = _rmsnorm(h, mix_norm_g[l], jnp.bfloat16)
        ya, ssq_a = _conv_mixer(xn, w_in[l], conv_w[l], conv_out_norm_g[l], seq, cw)
        yb, ssq_b, w_out_bf16 = _gmlp_mixer(xn, w_in[l], spatial_w[l], spatial_b[l],
                                            gmlp_out_norm_g[l], 3 * cw, gw, w_out[l])
        h, xg, ssq_h = _out_proj(ya, yb, w_out_bf16, ssq_a, ssq_b, h, mlp_norm_g[l])
        r, w_down_bf16 = _mlp_up(xg, w_up[l], ssq_h, w_down[l])
        h = _mlp_down(r, w_down_bf16, h)
    out = _rmsnorm(h, final_norm_g, x.dtype)
    return out.reshape(bsz, seq, d)
```
